```python
import math
import jax
import jax.numpy as jnp
from jax import lax
import numpy as np

D_MODEL = 1024
BATCH = 4
SEQ = 8192
DEPTH = 4

NORM_EPS = 1e-6
NEG_INF = -1e30
TINY = 1e-30

DN_HEADS = 4
DN_HEAD_DIM = 128
DN_CONV = 4
DN_CHUNK = 64
DN_W = DN_HEADS * DN_HEAD_DIM

MLA_HEADS = 4
MLA_Q_RANK = 256
MLA_KV_RANK = 128
MLA_NOPE = 128
MLA_ROPE = 64
MLA_V = 128
MLA_QK = MLA_NOPE + MLA_ROPE
MLA_W = MLA_HEADS * MLA_V
ROPE_THETA = 10000.0
Q_BLOCK = 128

EV_IN = 4 * DN_W + 2 * DN_HEADS + MLA_Q_RANK + MLA_KV_RANK + MLA_ROPE
EV_MIX = DN_W + MLA_W

NSA_HEADS = 16
NSA_GROUPS = 4
NSA_HPG = NSA_HEADS // NSA_GROUPS
NSA_HEAD_DIM = 64
NSA_Q_W = NSA_HEADS * NSA_HEAD_DIM
NSA_KV_W = NSA_GROUPS * NSA_HEAD_DIM
CMP_LEN = 32
CMP_STRIDE = 16
CMP_HIDDEN = 256
SLC_LEN = 64
SLC_TOPN = 16
WIN = 512
NSA_Q_BLOCK = 64
FORCE_SCORE = 1e9
OD_IN = NSA_Q_W + 6 * NSA_KV_W + 3 * NSA_HEADS

D_FF = 2816
N_EXPERTS = 8
TOP_K = 2
D_FF_EXPERT = 3584
MOE_BLOCK = 512

kernel_name = 'hybrid_deltanet_mla_nsa_moe_trunk'


def rms_norm(x, gain):
    xf = x.astype(jnp.float32)
    y = xf * lax.rsqrt(jnp.mean(xf * xf, axis=-1, keepdims=True) + NORM_EPS)
    return (y * gain.astype(jnp.float32)).astype(x.dtype)


def l2_normalize(x):
    return x * lax.rsqrt(jnp.sum(x * x, axis=-1, keepdims=True) + NORM_EPS)


def ada_modulation(c, w, b):
    m = jax.nn.silu(c) @ w + b
    shift, scale, gate = jnp.split(m[:, None, :], 3, axis=-1)
    return shift, scale, gate


def masked_softmax(s, mask):
    s = jnp.where(mask, s.astype(jnp.float32), NEG_INF)
    p = jnp.where(mask, jnp.exp(s - jnp.max(s, axis=-1, keepdims=True)), 0.0)
    return p / jnp.maximum(jnp.sum(p, axis=-1, keepdims=True), TINY)


def split_cols(t, sizes):
    return jnp.split(t, [int(v) for v in np.cumsum(sizes)[:-1]], axis=-1)


def causal_depthwise_conv(x, w):
    k_len, ch = w.shape
    return lax.conv_general_dilated(x, w[:, None, :].astype(x.dtype), window_strides=(1,),
                                    padding=[(k_len - 1, 0)],
                                    dimension_numbers=('NWC', 'WIO', 'NWC'),
                                    feature_group_count=ch)


def gated_delta_rule(q, k, v, g, beta):
    out_dtype = v.dtype
    f32 = jnp.float32
    b_, s_len, n_h, dk = q.shape
    dv = v.shape[-1]
    cs = DN_CHUNK
    n_ch = s_len // cs

    def to_chunks(t):
        return t.astype(f32).reshape(b_, n_ch, cs, n_h, -1).transpose(0, 3, 1, 2, 4)

    q = to_chunks(l2_normalize(q.astype(f32)) * (dk ** -0.5))
    k = to_chunks(l2_normalize(k.astype(f32)))
    v = to_chunks(v)
    g = jnp.cumsum(to_chunks(g[..., None])[..., 0], axis=-1)
    beta = to_chunks(beta[..., None])[..., 0]
    incl = jnp.tril(jnp.ones((cs, cs), dtype=bool))
    strict = jnp.tril(jnp.ones((cs, cs), dtype=bool), k=-1)
    gdiff = g[..., :, None] - g[..., None, :]
    decay = jnp.where(incl, jnp.exp(jnp.where(incl, gdiff, 0.0)), 0.0)
    k_beta = k * beta[..., None]
    lower = jnp.where(strict, jnp.einsum('bhnid,bhnjd->bhnij', k_beta, k) * decay, 0.0)
    tmat = lower + jnp.eye(cs, dtype=f32)
    u = lax.linalg.triangular_solve(tmat, v * beta[..., None], left_side=True, lower=True,
                                    unit_diagonal=True)
    w = lax.linalg.triangular_solve(tmat, k_beta * jnp.exp(g)[..., None], left_side=True,
                                    lower=True, unit_diagonal=True)
    qk = jnp.einsum('bhnid,bhnjd->bhnij', q, k) * decay
    q_g = q * jnp.exp(g)[..., None]
    k_tail = k * jnp.exp(g[..., -1:] - g)[..., None]
    a_last = jnp.exp(g[..., -1])

    def step(state, xs):
        u_i, w_i, qg_i, qk_i, kt_i, a_i = xs
        v_new = u_i - jnp.einsum('bhck,bhkv->bhcv', w_i, state)
        o_i = (jnp.einsum('bhck,bhkv->bhcv', qg_i, state)
               + jnp.einsum('bhij,bhjv->bhiv', qk_i, v_new))
        state = state * a_i[..., None, None] + jnp.einsum('bhck,bhcv->bhkv', kt_i, v_new)
        return state, o_i

    xs = tuple(jnp.moveaxis(t, 2, 0) for t in (u, w, q_g, qk, k_tail, a_last))
    state0 = jnp.zeros((b_, n_h, dk, dv), f32)
    _, o = lax.scan(step, state0, xs)
    return o.transpose(1, 0, 3, 2, 4).reshape(b_, s_len, n_h, dv).astype(out_dtype)


def gated_deltanet(p_q, p_k, p_v, p_z, p_b, p_a, conv_w, a_log, dt_bias, norm_gain):
    b_, s_len, _ = p_q.shape
    f32 = jnp.float32
    qkv = jax.nn.silu(causal_depthwise_conv(jnp.concatenate([p_q, p_k, p_v], axis=-1), conv_w))
    q, k, v = jnp.split(qkv, 3, axis=-1)

    def heads(t):
        return t.reshape(b_, s_len, DN_HEADS, DN_HEAD_DIM)

    beta = jax.nn.sigmoid(p_b.astype(f32))
    g = -jnp.exp(a_log.astype(f32)) * jax.nn.softplus(p_a.astype(f32) + dt_bias.astype(f32))
    o = gated_delta_rule(heads(q), heads(k), heads(v), g, beta)
    o = rms_norm(o, norm_gain) * jax.nn.silu(heads(p_z))
    return o.reshape(b_, s_len, DN_W)


def rope_tables(positions):
    inv = 1.0 / (ROPE_THETA ** (jnp.arange(0, MLA_ROPE, 2, dtype=jnp.float32) / MLA_ROPE))
    ang = positions.astype(jnp.float32)[..., None] * inv
    return jnp.cos(ang), jnp.sin(ang)


def apply_rope(x, cos, sin):
    half = x.shape[-1] // 2
    x1, x2 = x[..., :half], x[..., half:]
    cos = cos.astype(x.dtype)
    sin = sin.astype(x.dtype)
    return jnp.concatenate([x1 * cos - x2 * sin, x2 * cos + x1 * sin], axis=-1)


def causal_block_attention(q, k, v, scale):
    b_, s_len, n_h, _ = q.shape
    nb = s_len // Q_BLOCK
    q_b = q.reshape(b_, nb, Q_BLOCK, n_h, -1).swapaxes(0, 1)
    k_pos = jnp.arange(s_len)

    def block(args):
        i, q_i = args
        t = i * Q_BLOCK + jnp.arange(Q_BLOCK)
        s = jnp.einsum('bqhd,bkhd->bhqk', q_i, k).astype(jnp.float32) * scale
        s = jnp.where(k_pos[None, :] <= t[:, None], s, NEG_INF)
        p = jax.nn.softmax(s, axis=-1)
        return jnp.einsum('bhqk,bkhd->bqhd', p.astype(v.dtype), v)

    o = lax.map(block, (jnp.arange(nb), q_b))
    return o.swapaxes(0, 1).reshape(b_, s_len, n_h, v.shape[-1])


def multi_head_latent_attention(c_q, c_kv, k_r, cos, sin, q_norm, kv_norm, w_uq, w_ukv):
    b_, s_len, _ = c_q.shape
    q = (rms_norm(c_q, q_norm) @ w_uq).reshape(b_, s_len, MLA_HEADS, MLA_QK)
    kv = (rms_norm(c_kv, kv_norm) @ w_ukv).reshape(b_, s_len, MLA_HEADS, MLA_NOPE + MLA_V)
    q_nope, q_rope = q[..., :MLA_NOPE], q[..., MLA_NOPE:]
    k_nope, v = kv[..., :MLA_NOPE], kv[..., MLA_NOPE:]
    q_rope = apply_rope(q_rope, cos[:, :, None, :], sin[:, :, None, :])
    k_rope = apply_rope(k_r, cos, sin)[:, :, None, :]
    q = jnp.concatenate([q_nope, q_rope], axis=-1)
    k = jnp.concatenate([k_nope, jnp.broadcast_to(k_rope, (b_, s_len, MLA_HEADS, MLA_ROPE))], axis=-1)
    o = causal_block_attention(q, k, v, MLA_QK ** -0.5)
    return o.reshape(b_, s_len, MLA_W)


def even_token_mixer(h, cos, sin, w_in, conv_w, a_log, dt_bias, dn_norm, q_norm, kv_norm,
                     w_uq, w_ukv, w_out):
    sizes = [DN_W] * 4 + [DN_HEADS] * 2 + [MLA_Q_RANK, MLA_KV_RANK, MLA_ROPE]
    p_q, p_k, p_v, p_z, p_b, p_a, c_q, c_kv, k_r = split_cols(h @ w_in, sizes)
    o_a = gated_deltanet(p_q, p_k, p_v, p_z, p_b, p_a, conv_w, a_log, dt_bias, dn_norm)
    o_b = multi_head_latent_attention(c_q, c_kv, k_r, cos, sin, q_norm, kv_norm, w_uq, w_ukv)
    return jnp.concatenate([o_a, o_b], axis=-1) @ w_out


def compress_blocks(t, pos_emb, w1, w2):
    b_, s_len, n_g, d = t.shape
    r = CMP_LEN // CMP_STRIDE
    n_chunk = s_len // CMP_STRIDE
    n_cmp = n_chunk - r + 1
    ch = t.reshape(b_, n_chunk, CMP_STRIDE, n_g, d)
    blocks = jnp.concatenate([ch[:, j:j + n_cmp] for j in range(r)], axis=2)
    blocks = blocks + pos_emb[:, None, :]
    flat = blocks.transpose(0, 1, 3, 2, 4).reshape(b_, n_cmp, n_g, CMP_LEN * d)
    return jax.nn.silu(flat @ w1) @ w2


def selection_importance(p_cmp, n_slc):
    r = CMP_LEN // CMP_STRIDE
    pad = [(0, 0)] * (p_cmp.ndim - 1)
    chunk = sum(jnp.pad(p_cmp, pad + [(j, r - 1 - j)]) for j in range(r)) / r
    return chunk.reshape(*chunk.shape[:-1], n_slc, SLC_LEN // CMP_STRIDE).sum(axis=-1)


def native_sparse_attention(q, k_cmp, v_cmp, k_slc, v_slc, k_win, v_win, gates):
    b_, s_len, n_g, n_hg, d = q.shape
    qb = NSA_Q_BLOCK
    nb = s_len // qb
    n_slc = s_len // SLC_LEN
    n_top = min(SLC_TOPN, n_slc)
    scale = d ** -0.5
    cmp_end = jnp.arange(k_cmp.shape[1]) * CMP_STRIDE + CMP_LEN - 1
    ks_blk = k_slc.reshape(b_, n_slc, SLC_LEN, n_g, d).transpose(0, 3, 1, 2, 4)
    vs_blk = v_slc.reshape(b_, n_slc, SLC_LEN, n_g, d).transpose(0, 3, 1, 2, 4)
    kw_pad = jnp.pad(k_win, ((0, 0), (WIN, 0), (0, 0), (0, 0)))
    vw_pad = jnp.pad(v_win, ((0, 0), (WIN, 0), (0, 0), (0, 0)))
    b_idx = jnp.arange(b_)[:, None, None, None]
    g_idx = jnp.arange(n_g)[None, :, None, None]
    blk_ids = jnp.arange(n_slc)
    m_sel = n_top * SLC_LEN

    def block(args):
        i, q_i, gate_i = args
        t = i * qb + jnp.arange(qb)
        s_c = jnp.einsum('bqghd,bngd->bghqn', q_i, k_cmp) * scale
        p_c = masked_softmax(s_c, cmp_end[None, :] <= t[:, None])
        o_c = jnp.einsum('bghqn,bngd->bqghd', p_c.astype(v_cmp.dtype), v_cmp)
        imp = selection_importance(jnp.sum(p_c, axis=2), n_slc)
        cur = (t // SLC_LEN)[:, None]
        forced = (blk_ids == 0) | (blk_ids == cur) | (blk_ids == cur - 1)
        imp = jnp.where(forced, FORCE_SCORE, imp)
        imp = jnp.where(blk_ids <= cur, imp, -1.0)
        _, sel = lax.top_k(imp, n_top)
        k_sel = ks_blk[b_idx, g_idx, sel].reshape(b_, n_g, qb, m_sel, d)
        v_sel = vs_blk[b_idx, g_idx, sel].reshape(b_, n_g, qb, m_sel, d)
        key_pos = (sel[..., None] * SLC_LEN + jnp.arange(SLC_LEN)).reshape(b_, n_g, qb, m_sel)
        s_s = jnp.einsum('bqghd,bgqmd->bghqm', q_i, k_sel) * scale
        p_s = masked_softmax(s_s, (key_pos <= t[:, None])[:, :, None])
        o_s = jnp.einsum('bghqm,bgqmd->bqghd', p_s.astype(v_sel.dtype), v_sel)
        kw = lax.dynamic_slice_in_dim(kw_pad, i * qb, qb + WIN, axis=1)
        vw = lax.dynamic_slice_in_dim(vw_pad, i * qb, qb + WIN, axis=1)
        w_pos = i * qb - WIN + jnp.arange(qb + WIN)
        diff = t[:, None] - w_pos[None, :]
        m_w = (diff >= 0) & (diff < WIN) & (w_pos[None, :] >= 0)
        p_w = masked_softmax(jnp.einsum('bqghd,bkgd->bghqk', q_i, kw) * scale, m_w)
        o_w = jnp.einsum('bghqk,bkgd->bqghd', p_w.astype(vw.dtype), vw)
        return gate_i[..., 0:1] * o_c + gate_i[..., 1:2] * o_s + gate_i[..., 2:3] * o_w

    q_b = q.reshape(b_, nb, qb, n_g, n_hg, d).swapaxes(0, 1)
    g_b = gates.reshape(b_, nb, qb, n_g, n_hg, 3).swapaxes(0, 1)
    o = lax.map(block, (jnp.arange(nb), q_b, g_b))
    return o.swapaxes(0, 1).reshape(b_, s_len, n_g * n_hg * d)


def odd_token_mixer(h, w_in, pos_k, pos_v, ck1, ck2, cv1, cv2, w_out):
    b_, s_len, _ = h.shape
    sizes = [NSA_Q_W] + [NSA_KV_W] * 6 + [3 * NSA_HEADS]
    q, kc, vc, ks, vs, kw, vw, gl = split_cols(h @ w_in, sizes)

    def kvh(t):
        return t.reshape(b_, s_len, NSA_GROUPS, NSA_HEAD_DIM)

    o = native_sparse_attention(
        q.reshape(b_, s_len, NSA_GROUPS, NSA_HPG, NSA_HEAD_DIM),
        compress_blocks(kvh(kc), pos_k, ck1, ck2),
        compress_blocks(kvh(vc), pos_v, cv1, cv2),
        kvh(ks), kvh(vs), kvh(kw), kvh(vw),
        jax.nn.sigmoid(gl).reshape(b_, s_len, NSA_GROUPS, NSA_HPG, 3))
    return o @ w_out


def swiglu(h, w_gate, w_up, w_down):
    return (jax.nn.silu(h @ w_gate) * (h @ w_up)) @ w_down


def moe_swiglu(h, w_router, b_router, w1, w3, w2):
    b_, s_len, dm = h.shape
    x = h.reshape(-1, dm)
    n_tok = x.shape[0]
    logits = (x @ w_router).astype(jnp.float32) + b_router.astype(jnp.float32)
    top_val, top_idx = lax.top_k(logits, TOP_K)
    gate_w = jax.nn.softmax(top_val, axis=-1)
    n_assign = n_tok * TOP_K
    flat_e = top_idx.reshape(-1)
    flat_tok = jnp.repeat(jnp.arange(n_tok, dtype=jnp.int32), TOP_K)
    order = jnp.argsort(flat_e)
    s_e = flat_e[order]
    s_tok = flat_tok[order]
    s_w = gate_w.reshape(-1)[order]
    counts = jnp.zeros((N_EXPERTS,), jnp.int32).at[flat_e].add(1)
    padded = ((counts + MOE_BLOCK - 1) // MOE_BLOCK) * MOE_BLOCK
    pad_end = jnp.cumsum(padded)
    pad_start = pad_end - padded
    start = jnp.cumsum(counts) - counts
    dest = pad_start[s_e] + jnp.arange(n_assign, dtype=jnp.int32) - start[s_e]
    n_blk = -(-n_assign // MOE_BLOCK) + N_EXPERTS
    n_rows = n_blk * MOE_BLOCK
    row_tok = jnp.zeros((n_rows,), jnp.int32).at[dest].set(s_tok)
    blk_exp = jnp.minimum(jnp.searchsorted(pad_end, jnp.arange(n_blk, dtype=jnp.int32) * MOE_BLOCK,
                                           side='right'), N_EXPERTS - 1)
    x_rows = x[row_tok].reshape(n_blk, MOE_BLOCK, dm)

    def expert_block(args):
        xb, e = args
        return swiglu(xb, w1[e], w3[e], w2[e])

    y_rows = lax.map(expert_block, (x_rows, blk_exp)).reshape(n_rows, dm)
    y = jnp.zeros_like(x).at[s_tok].add(y_rows[dest] * s_w[:, None].astype(y_rows.dtype))
    return y.reshape(b_, s_len, dm)


def setup_inputs(seed: int = 0) -> dict:
    key = jax.random.key(seed)
    keys = list(jax.random.split(key, 48))
    f32 = jnp.float32
    n_ev = (DEPTH + 1) // 2
    n_od = DEPTH // 2

    def nk():
        return keys.pop()

    def dense(shape, fan_in, gain=1.0):
        return jax.random.normal(nk(), shape, f32) * (gain * fan_in ** -0.5)

    def gain_init(shape):
        return 1.0 + 0.02 * jax.random.normal(nk(), shape, f32)

    x = jax.random.normal(nk(), (BATCH, SEQ, D_MODEL), f32)
    c = jax.random.normal(nk(), (BATCH, D_MODEL), f32)
    offset = jax.random.randint(nk(), (BATCH, 1), 0, 4096, dtype=jnp.int32)
    positions = offset + jnp.arange(SEQ, dtype=jnp.int32)[None, :]
    dt = jnp.exp(jax.random.uniform(nk(), (n_ev, DN_HEADS), f32, math.log(1e-3), math.log(1e-1)))
    return {
        'x': x,
        'c': c,
        'positions': positions,
        'ada_w': dense((DEPTH, 2, D_MODEL, 3 * D_MODEL), D_MODEL, 0.5),
        'ada_b': 0.02 * jax.random.normal(nk(), (DEPTH, 2, 3 * D_MODEL), f32),
        'norm_g': gain_init((DEPTH, 2, D_MODEL)),
        'final_g': gain_init((D_MODEL,)),
        'ev_w_in': dense((n_ev, D_MODEL, EV_IN), D_MODEL),
        'ev_conv_w': dense((n_ev, DN_CONV, 3 * DN_W), DN_CONV),
        'ev_a_log': jnp.log(jax.random.uniform(nk(), (n_ev, DN_HEADS), f32, 1.0, 16.0)),
        'ev_dt_bias': dt + jnp.log(-jnp.expm1(-dt)),
        'ev_dn_norm': gain_init((n_ev, DN_HEAD_DIM)),
        'ev_q_norm': gain_init((n_ev, MLA_Q_RANK)),
        'ev_kv_norm': gain_init((n_ev, MLA_KV_RANK)),
        'ev_w_uq': dense((n_ev, MLA_Q_RANK, MLA_HEADS * MLA_QK), MLA_Q_RANK),
        'ev_w_ukv': dense((n_ev, MLA_KV_RANK, MLA_HEADS * (MLA_NOPE + MLA_V)), MLA_KV_RANK),
        'ev_w_out': dense((n_ev, EV_MIX, D_MODEL), EV_MIX),
        'ev_ff_gate': dense((n_ev, D_MODEL, D_FF), D_MODEL),
        'ev_ff_up': dense((n_ev, D_MODEL, D_FF), D_MODEL),
        'ev_ff_down': dense((n_ev, D_FF, D_MODEL), D_FF),
        'od_w_in': dense((n_od, D_MODEL, OD_IN), D_MODEL),
        'od_cmp_pos_k': 0.1 * jax.random.normal(nk(), (n_od, CMP_LEN, NSA_HEAD_DIM), f32),
        'od_cmp_pos_v': 0.1 * jax.random.normal(nk(), (n_od, CMP_LEN, NSA_HEAD_DIM), f32),
        'od_cmp_k1': dense((n_od, CMP_LEN * NSA_HEAD_DIM, CMP_HIDDEN), CMP_LEN * NSA_HEAD_DIM),
        'od_cmp_k2': dense((n_od, CMP_HIDDEN, NSA_HEAD_DIM), CMP_HIDDEN),
        'od_cmp_v1': dense((n_od, CMP_LEN * NSA_HEAD_DIM, CMP_HIDDEN), CMP_LEN * NSA_HEAD_DIM),
        'od_cmp_v2': dense((n_od, CMP_HIDDEN, NSA_HEAD_DIM), CMP_HIDDEN),
        'od_w_out': dense((n_od, NSA_Q_W, D_MODEL), NSA_Q_W),
        'od_router': dense((n_od, D_MODEL, N_EXPERTS), D_MODEL),
        'od_router_b': 0.01 * jax.random.normal(nk(), (n_od, N_EXPERTS), f32),
        'od_moe_w1': dense((n_od, N_EXPERTS, D_MODEL, D_FF_EXPERT), D_MODEL),
        'od_moe_w3': dense((n_od, N_EXPERTS, D_MODEL, D_FF_EXPERT), D_MODEL),
        'od_moe_w2': dense((n_od, N_EXPERTS, D_FF_EXPERT, D_MODEL), D_FF_EXPERT),
    }


def reference(x, c, positions, ada_w, ada_b, norm_g, final_g,
              ev_w_in, ev_conv_w, ev_a_log, ev_dt_bias, ev_dn_norm, ev_q_norm, ev_kv_norm,
              ev_w_uq, ev_w_ukv, ev_w_out, ev_ff_gate, ev_ff_up, ev_ff_down,
              od_w_in, od_cmp_pos_k, od_cmp_pos_v, od_cmp_k1, od_cmp_k2, od_cmp_v1, od_cmp_v2,
              od_w_out, od_router, od_router_b, od_moe_w1, od_moe_w3, od_moe_w2):
    cos, sin = rope_tables(positions)
    for layer in range(DEPTH):
        j = layer // 2
        shift, scale, gate = ada_modulation(c, ada_w[layer, 0], ada_b[layer, 0])
        h = rms_norm(x, norm_g[layer, 0]) * (1.0 + scale) + shift
        if layer % 2 == 0:
            y = even_token_mixer(h, cos, sin, ev_w_in[j], ev_conv_w[j], ev_a_log[j], ev_dt_bias[j],
                                 ev_dn_norm[j], ev_q_norm[j], ev_kv_norm[j], ev_w_uq[j],
                                 ev_w_ukv[j], ev_w_out[j])
        else:
            y = odd_token_mixer(h, od_w_in[j], od_cmp_pos_k[j], od_cmp_pos_v[j], od_cmp_k1[j],
                                od_cmp_k2[j], od_cmp_v1[j], od_cmp_v2[j], od_w_out[j])
        x = x + gate * y
        shift, scale, gate = ada_modulation(c, ada_w[layer, 1], ada_b[layer, 1])
        h = rms_norm(x, norm_g[layer, 1]) * (1.0 + scale) + shift
        if layer % 2 == 0:
            y = swiglu(h, ev_ff_gate[j], ev_ff_up[j], ev_ff_down[j])
        else:
            y = moe_swiglu(h, od_router[j], od_router_b[j], od_moe_w1[j], od_moe_w3[j], od_moe_w2[j])
        x = x + gate * y
    return rms_norm(x, final_g)
```

```python
import functools
import math

import jax
import jax.numpy as jnp
import numpy as np
from jax import lax
from jax.experimental import pallas as pl
from jax.experimental.pallas import tpu as pltpu

F32 = jnp.float32
BF16 = jnp.bfloat16

NORM_EPS = 1e-6
NEG_INF = -1e30
TINY = 1e-30

LANES = 128

DN_HEADS = 4
DN_HEAD_DIM = 128
DN_CONV = 4
DN_CHUNK = 64
DN_W = DN_HEADS * DN_HEAD_DIM

MLA_HEADS = 4
MLA_Q_RANK = 256
MLA_KV_RANK = 128
MLA_NOPE = 128
MLA_ROPE = 64
MLA_V = 128
MLA_QK = MLA_NOPE + MLA_ROPE
ROPE_THETA = 10000.0

NSA_HEADS = 16
NSA_GROUPS = 4
NSA_HPG = NSA_HEADS // NSA_GROUPS
NSA_HEAD_DIM = 64
NSA_Q_W = NSA_HEADS * NSA_HEAD_DIM
NSA_KV_W = NSA_GROUPS * NSA_HEAD_DIM
CMP_LEN = 32
CMP_STRIDE = 16
CMP_HIDDEN = 256
SLC_LEN = 64
SLC_TOPN = 16
WIN = 512
FORCE_SCORE = 1e9

N_EXPERTS = 8
TOP_K = 2
MOE_BLOCK = 512

VMEM_LIMIT = 56 * 1024 * 1024


def _params(sem):
    return pltpu.CompilerParams(dimension_semantics=sem, vmem_limit_bytes=VMEM_LIMIT)


def _sigmoid(x):
    return 1.0 / (1.0 + jnp.exp(-x))


def _silu(x):
    return x * _sigmoid(x)


def _dot(a, b):
    return jnp.dot(a.astype(BF16), b.astype(BF16), preferred_element_type=F32)


def _dot_nt(a, b):
    return lax.dot_general(a.astype(BF16), b.astype(BF16), (((1,), (1,)), ((), ())),
                           preferred_element_type=F32)


def _dot_tn(a, b):
    return lax.dot_general(a.astype(BF16), b.astype(BF16), (((0,), (0,)), ((), ())),
                           preferred_element_type=F32)


def _split3(x):
    hi = x.astype(BF16)
    r1 = x - hi.astype(F32)
    mid = r1.astype(BF16)
    lo = (r1 - mid.astype(F32)).astype(BF16)
    return hi, mid, lo


def _dot_lhs01(a01, x):
    hi, mid, lo = _split3(x)
    a = a01.astype(BF16)
    d = functools.partial(jnp.dot, preferred_element_type=F32)
    return d(a, hi) + d(a, mid) + d(a, lo)


def _dot_rhs01(x, b01):
    hi, mid, lo = _split3(x)
    b = b01.astype(BF16)
    d = functools.partial(jnp.dot, preferred_element_type=F32)
    return d(hi, b) + d(mid, b) + d(lo, b)


def _dot_hi(a, b):
    ah = a.astype(BF16)
    al = (a - ah.astype(F32)).astype(BF16)
    bh = b.astype(BF16)
    bl = (b - bh.astype(F32)).astype(BF16)
    d = functools.partial(jnp.dot, preferred_element_type=F32)
    return d(ah, bh) + d(ah, bl) + d(al, bh)


def _rms(x, gain):
    return x * lax.rsqrt(jnp.mean(x * x, axis=-1, keepdims=True) + NORM_EPS) * gain


def _mm_body(*refs, pro, epi):
    x_ref, w_ref = refs[0], refs[1]
    pos = 2
    if pro == 'adaln':
        g_ref, sc_ref, sh_ref = refs[pos:pos + 3]
        pos += 3
    elif pro == 'rms':
        g_ref = refs[pos]
        pos += 1
    if epi == 'residual':
        res_ref, gate_ref = refs[pos:pos + 2]
        pos += 2
    o_ref = refs[pos]
    pos += 1
    if pro is not None:
        h_ref = refs[pos]

        @pl.when(pl.program_id(1) == 0)
        def _():
            y = _rms(x_ref[...].astype(F32), g_ref[...])
            if pro == 'adaln':
                y = y * (1.0 + sc_ref[...]) + sh_ref[...]
            h_ref[...] = y.astype(BF16)

        h = h_ref[...]
    else:
        h = x_ref[...].astype(BF16)
    acc = jnp.dot(h, w_ref[...], preferred_element_type=F32)
    if epi == 'residual':
        acc = res_ref[...] + gate_ref[...] * acc
    o_ref[...] = acc.astype(o_ref.dtype)


def _pick_tile(n, cap):
    best = LANES
    t = LANES
    while t <= min(n, cap):
        if n % t == 0:
            best = t
        t += LANES
    return best


def _mm(x, w, *, pro=None, pro_args=(), epi=None, epi_args=(), rows_per_batch=None,
        out_dtype=F32, tm=512, tn_cap=1024, name='mm'):
    m, k = x.shape
    n = w.shape[1]
    tm = min(tm, m)
    tn = _pick_tile(n, tn_cap)
    assert m % tm == 0 and n % tn == 0
    rpb = None if rows_per_batch is None else rows_per_batch // tm
    in_specs = [pl.BlockSpec((tm, k), lambda i, j: (i, 0)),
                pl.BlockSpec((k, tn), lambda i, j: (0, j))]
    args = [x, w.astype(BF16)]
    scratch = []
    if pro == 'adaln':
        g, sc, sh = pro_args
        in_specs += [pl.BlockSpec((1, k), lambda i, j: (0, 0)),
                     pl.BlockSpec((None, 1, k), lambda i, j: (i // rpb, 0, 0)),
                     pl.BlockSpec((None, 1, k), lambda i, j: (i // rpb, 0, 0))]
        args += [g.reshape(1, k), sc, sh]
    elif pro == 'rms':
        in_specs += [pl.BlockSpec((1, k), lambda i, j: (0, 0))]
        args += [pro_args[0].reshape(1, k)]
    if pro is not None:
        scratch = [pltpu.VMEM((tm, k), BF16)]
    if epi == 'residual':
        res, gate = epi_args
        in_specs += [pl.BlockSpec((tm, tn), lambda i, j: (i, j)),
                     pl.BlockSpec((None, 1, tn), lambda i, j: (i // rpb, 0, j))]
        args += [res, gate]
    return pl.pallas_call(
        functools.partial(_mm_body, pro=pro, epi=epi),
        out_shape=jax.ShapeDtypeStruct((m, n), out_dtype),
        grid=(m // tm, n // tn),
        in_specs=in_specs,
        out_specs=pl.BlockSpec((tm, tn), lambda i, j: (i, j)),
        scratch_shapes=scratch,
        compiler_params=_params(("parallel", "arbitrary")),
        name=name,
    )(*args)


def _swiglu_dense_body(x_ref, g_ref, sc_ref, sh_ref, wg_ref, wu_ref, wd_ref, gate_ref, o_ref,
                       h_ref, acc_ref):
    j = pl.program_id(1)

    @pl.when(j == 0)
    def _():
        y = _rms(x_ref[...], g_ref[...]) * (1.0 + sc_ref[...]) + sh_ref[...]
        h_ref[...] = y.astype(BF16)
        acc_ref[...] = jnp.zeros_like(acc_ref)

    h = h_ref[...]
    a = jnp.dot(h, wg_ref[...], preferred_element_type=F32)
    b = jnp.dot(h, wu_ref[...], preferred_element_type=F32)
    act = (_silu(a) * b).astype(BF16)
    acc_ref[...] += jnp.dot(act, wd_ref[...], preferred_element_type=F32)

    @pl.when(j == pl.num_programs(1) - 1)
    def _():
        o_ref[...] = x_ref[...] + gate_ref[...] * acc_ref[...]


def _swiglu_dense(x, g, sc, sh, wg, wu, wd, gate, rows_per_batch, *, tm=1024, tf=256):
    m, k = x.shape
    f = wg.shape[1]
    assert m % tm == 0 and f % tf == 0
    rpb = rows_per_batch // tm
    return pl.pallas_call(
        _swiglu_dense_body,
        out_shape=jax.ShapeDtypeStruct((m, k), F32),
        grid=(m // tm, f // tf),
        in_specs=[pl.BlockSpec((tm, k), lambda i, j: (i, 0)),
                  pl.BlockSpec((1, k), lambda i, j: (0, 0)),
                  pl.BlockSpec((None, 1, k), lambda i, j: (i // rpb, 0, 0)),
                  pl.BlockSpec((None, 1, k), lambda i, j: (i // rpb, 0, 0)),
                  pl.BlockSpec((k, tf), lambda i, j: (0, j)),
                  pl.BlockSpec((k, tf), lambda i, j: (0, j)),
                  pl.BlockSpec((tf, k), lambda i, j: (j, 0)),
                  pl.BlockSpec((None, 1, k), lambda i, j: (i // rpb, 0, 0))],
        out_specs=pl.BlockSpec((tm, k), lambda i, j: (i, 0)),
        scratch_shapes=[pltpu.VMEM((tm, k), BF16), pltpu.VMEM((tm, k), F32)],
        compiler_params=_params(("parallel", "arbitrary")),
        name='swiglu_dense',
    )(x, g.reshape(1, k), sc, sh, wg.astype(BF16), wu.astype(BF16), wd.astype(BF16), gate)


def _swiglu_grouped_body(be_ref, nu_ref, x_ref, wg_ref, wu_ref, wd_ref, o_ref, acc_ref):
    i = pl.program_id(0)
    j = pl.program_id(1)
    used = i < nu_ref[0]

    @pl.when(j == 0)
    def _():
        acc_ref[...] = jnp.zeros_like(acc_ref)

    @pl.when(used)
    def _():
        h = x_ref[...]
        a = jnp.dot(h, wg_ref[...], preferred_element_type=F32)
        b = jnp.dot(h, wu_ref[...], preferred_element_type=F32)
        act = (_silu(a) * b).astype(BF16)
        acc_ref[...] += jnp.dot(act, wd_ref[...], preferred_element_type=F32)

    @pl.when(j == pl.num_programs(1) - 1)
    def _():
        o_ref[...] = acc_ref[...]


def _swiglu_grouped(x_rows, blk_exp, n_used, w1, w3, w2, *, tm=MOE_BLOCK, tf=512):
    n_rows, k = x_rows.shape
    f = w1.shape[2]
    assert n_rows % tm == 0 and f % tf == 0
    nj = f // tf

    def jj(i, j, nu):
        return jnp.where(i < nu[0], j, nj - 1)

    grid_spec = pltpu.PrefetchScalarGridSpec(
        num_scalar_prefetch=2,
        grid=(n_rows // tm, nj),
        in_specs=[pl.BlockSpec((tm, k), lambda i, j, be, nu: (i, 0)),
                  pl.BlockSpec((None, k, tf), lambda i, j, be, nu: (be[i], 0, jj(i, j, nu))),
                  pl.BlockSpec((None, k, tf), lambda i, j, be, nu: (be[i], 0, jj(i, j, nu))),
                  pl.BlockSpec((None, tf, k), lambda i, j, be, nu: (be[i], jj(i, j, nu), 0))],
        out_specs=pl.BlockSpec((tm, k), lambda i, j, be, nu: (i, 0)),
        scratch_shapes=[pltpu.VMEM((tm, k), F32)],
    )
    return pl.pallas_call(
        _swiglu_grouped_body,
        out_shape=jax.ShapeDtypeStruct((n_rows, k), F32),
        grid_spec=grid_spec,
        compiler_params=_params(("arbitrary", "arbitrary")),
        name='swiglu_grouped',
    )(blk_exp, n_used, x_rows, w1.astype(BF16), w3.astype(BF16), w2.astype(BF16))


def _norm_router_body(x_ref, g_ref, sc_ref, sh_ref, wr_ref, h_ref, lg_ref):
    y = _rms(x_ref[...], g_ref[...]) * (1.0 + sc_ref[...]) + sh_ref[...]
    h_ref[...] = y.astype(BF16)
    lg_ref[...] = jnp.dot(y, wr_ref[...], preferred_element_type=F32,
                          precision=lax.Precision.HIGHEST)


def _norm_router(x, g, sc, sh, w_router, rows_per_batch, *, tm=512):
    m, k = x.shape
    e = w_router.shape[1]
    wr = jnp.zeros((k, LANES), F32).at[:, :e].set(w_router)
    rpb = rows_per_batch // tm
    return pl.pallas_call(
        _norm_router_body,
        out_shape=(jax.ShapeDtypeStruct((m, k), BF16), jax.ShapeDtypeStruct((m, LANES), F32)),
        grid=(m // tm,),
        in_specs=[pl.BlockSpec((tm, k), lambda i: (i, 0)),
                  pl.BlockSpec((1, k), lambda i: (0, 0)),
                  pl.BlockSpec((None, 1, k), lambda i: (i // rpb, 0, 0)),
                  pl.BlockSpec((None, 1, k), lambda i: (i // rpb, 0, 0)),
                  pl.BlockSpec((k, LANES), lambda i: (0, 0))],
        out_specs=(pl.BlockSpec((tm, k), lambda i: (i, 0)),
                   pl.BlockSpec((tm, LANES), lambda i: (i, 0))),
        compiler_params=_params(("parallel",)),
        name='norm_router',
    )(x, g.reshape(1, k), sc, sh, wr)


def _final_norm_body(x_ref, g_ref, o_ref):
    o_ref[...] = _rms(x_ref[...], g_ref[...])


def _final_norm(x, g, *, tm=1024):
    m, k = x.shape
    return pl.pallas_call(
        _final_norm_body,
        out_shape=jax.ShapeDtypeStruct((m, k), F32),
        grid=(m // tm,),
        in_specs=[pl.BlockSpec((tm, k), lambda i: (i, 0)), pl.BlockSpec((1, k), lambda i: (0, 0))],
        out_specs=pl.BlockSpec((tm, k), lambda i: (i, 0)),
        compiler_params=_params(("parallel",)),
        name='final_norm',
    )(x, g.reshape(1, k))


def _softplus(x):
    return jnp.maximum(x, 0.0) + jnp.log(1.0 + jnp.exp(-jnp.abs(x)))


def _deltanet_body(q_ref, k_ref, v_ref, z_ref, pb_ref, pa_ref, alog_ref, dtb_ref, gn_ref, o_ref,
                   s_ref, *, n_chunks):
    c_len = DN_CHUNK

    @pl.when(pl.program_id(1) == 0)
    def _():
        s_ref[...] = jnp.zeros_like(s_ref)

    row = lax.broadcasted_iota(jnp.int32, (c_len, c_len), 0)
    col = lax.broadcasted_iota(jnp.int32, (c_len, c_len), 1)
    incl = row >= col
    strict = row > col
    incl_f = incl.astype(F32)
    strict_f = strict.astype(F32)
    eye = (row == col).astype(F32)
    neg_a = -jnp.exp(alog_ref[...])
    state = s_ref[...]
    for c in range(n_chunks):
        sl = pl.ds(c * c_len, c_len)
        q = q_ref[sl, :]
        k = k_ref[sl, :]
        v = v_ref[sl, :]
        q = q * lax.rsqrt(jnp.sum(q * q, axis=-1, keepdims=True) + NORM_EPS) * (DN_HEAD_DIM ** -0.5)
        k = k * lax.rsqrt(jnp.sum(k * k, axis=-1, keepdims=True) + NORM_EPS)
        beta = _sigmoid(pb_ref[sl, :])
        g = neg_a * _softplus(pa_ref[sl, :] + dtb_ref[...])
        gc = _dot_lhs01(incl_f, g)
        gdiff = _dot_lhs01(incl_f, g[:, :c_len] * strict_f)
        decay = jnp.where(incl, jnp.exp(jnp.where(incl, gdiff, 0.0)), 0.0)
        eg = jnp.exp(gc)
        kb = k * beta
        lower = jnp.where(strict, _dot_nt(kb, k) * decay, 0.0)
        tinv = eye - lower
        pw = lower
        for _ in range(5):
            pw = _dot_hi(pw, pw)
            tinv = tinv + _dot_hi(tinv, pw)
        u = _dot(tinv, v * beta)
        w = _dot(tinv, kb * eg)
        qk = _dot_nt(q, k) * decay
        g_last = gc[c_len - 1:c_len, :]
        k_tail = k * jnp.exp(g_last - gc)
        v_new = u - _dot(w, state)
        o = _dot(q * eg, state) + _dot(qk, v_new)
        state = state * jnp.exp(g_last) + _dot_tn(k_tail, v_new)
        z = z_ref[sl, :]
        o_ref[sl, :] = (_rms(o, gn_ref[...]) * _silu(z)).astype(o_ref.dtype)
    s_ref[...] = state


def _deltanet(qkv, proj, a_log, dt_bias, dn_norm, *, z_blk, pb_blk, pa_blk, tile=512):
    b, s, _ = qkv.shape
    h = DN_HEADS
    d = DN_HEAD_DIM
    tile = min(tile, s)
    assert s % tile == 0 and tile % DN_CHUNK == 0
    rep = lambda t: jnp.broadcast_to(t.astype(F32)[:, None, None], (h, 1, d))

    def col(off):
        return pl.BlockSpec((None, tile, d), lambda bh, t, off=off: (bh // h, t, off + bh % h))

    per_head = pl.BlockSpec((None, 1, d), lambda bh, t: (bh % h, 0, 0))
    return pl.pallas_call(
        functools.partial(_deltanet_body, n_chunks=tile // DN_CHUNK),
        out_shape=jax.ShapeDtypeStruct((b, s, h * d), BF16),
        grid=(b * h, s // tile),
        in_specs=[col(0), col(h), col(2 * h), col(z_blk), col(pb_blk), col(pa_blk),
                  per_head, per_head, pl.BlockSpec((1, d), lambda bh, t: (0, 0))],
        out_specs=col(0),
        scratch_shapes=[pltpu.VMEM((d, d), F32)],
        compiler_params=_params(("parallel", "arbitrary")),
        name='deltanet',
    )(qkv, qkv, qkv, proj, proj, proj, rep(a_log), rep(dt_bias), dn_norm.reshape(1, d))


def _mla_body(q_ref, k_ref, v_ref, o_ref, *, tq, tk):
    qi = pl.program_id(2)
    q = q_ref[...]

    def step(kt, carry, masked):
        m, l, acc = carry
        ks = pl.multiple_of(kt * tk, tk)
        k = k_ref[pl.ds(ks, tk), :]
        v = v_ref[pl.ds(ks, tk), :]
        s = lax.dot_general(q, k, (((1,), (1,)), ((), ())), preferred_element_type=F32)
        if masked:
            tpos = qi * tq + lax.broadcasted_iota(jnp.int32, (tq, tk), 0)
            kpos = ks + lax.broadcasted_iota(jnp.int32, (tq, tk), 1)
            s = jnp.where(kpos <= tpos, s, NEG_INF)
        m_new = jnp.maximum(m, jnp.max(s, axis=-1, keepdims=True))
        alpha = jnp.exp(m - m_new)
        p = jnp.exp(s - m_new)
        l = alpha * l + jnp.sum(p, axis=-1, keepdims=True)
        acc = alpha * acc + jnp.dot(p.astype(BF16), v, preferred_element_type=F32)
        return m_new, l, acc

    init = (jnp.full((tq, 1), NEG_INF, F32), jnp.zeros((tq, 1), F32),
            jnp.zeros((tq, v_ref.shape[-1]), F32))
    n_full = qi * (tq // tk)
    carry = lax.fori_loop(0, n_full, lambda kt, c: step(kt, c, False), init)
    for d in range(tq // tk):
        carry = step(n_full + d, carry, True)
    _, l, acc = carry
    o_ref[...] = (acc / l).astype(o_ref.dtype)


def _mla_attention(q, k, v, *, tq=512, tk=512):
    b, h, s, dqk = q.shape
    dv = v.shape[-1]
    tq = min(tq, s)
    tk = min(tk, tq)
    assert s % tq == 0 and tq % tk == 0
    return pl.pallas_call(
        functools.partial(_mla_body, tq=tq, tk=tk),
        out_shape=jax.ShapeDtypeStruct((b, s, h * dv), BF16),
        grid=(b, h, s // tq),
        in_specs=[pl.BlockSpec((None, None, tq, dqk), lambda bi, hi, qi: (bi, hi, qi, 0)),
                  pl.BlockSpec((None, None, s, dqk), lambda bi, hi, qi: (bi, hi, 0, 0)),
                  pl.BlockSpec((None, None, s, dv), lambda bi, hi, qi: (bi, hi, 0, 0))],
        out_specs=pl.BlockSpec((None, tq, dv), lambda bi, hi, qi: (bi, qi, hi)),
        compiler_params=_params(("parallel", "parallel", "arbitrary")),
        name='mla_attention',
    )(q, k, v)


def _compress_body(x_ref, pos_ref, w1_ref, w2_ref, o_ref):
    x = (x_ref[...] + pos_ref[...]).astype(BF16)
    hid = _silu(jnp.dot(x, w1_ref[...], preferred_element_type=F32))
    o_ref[...] = jnp.dot(hid.astype(BF16), w2_ref[...], preferred_element_type=F32)


def _compress(t, pos_emb, w1, w2, *, tm=512):
    b, s, g, d = t.shape
    n_chunk = s // CMP_STRIDE
    ch = t.reshape(b, n_chunk, CMP_STRIDE, g, d).transpose(0, 3, 1, 2, 4).reshape(b, g, n_chunk, CMP_STRIDE * d)
    nxt = jnp.concatenate([ch[:, :, 1:], jnp.zeros_like(ch[:, :, :1])], axis=2)
    x = jnp.concatenate([ch, nxt], axis=-1).reshape(b * g * n_chunk, CMP_LEN * d)
    m, k = x.shape
    tm = min(tm, m)
    w2p = jnp.zeros((CMP_HIDDEN, LANES), BF16).at[:, :d].set(w2.astype(BF16))
    out = pl.pallas_call(
        _compress_body,
        out_shape=jax.ShapeDtypeStruct((m, LANES), F32),
        grid=(m // tm,),
        in_specs=[pl.BlockSpec((tm, k), lambda i: (i, 0)),
                  pl.BlockSpec((1, k), lambda i: (0, 0)),
                  pl.BlockSpec((k, CMP_HIDDEN), lambda i: (0, 0)),
                  pl.BlockSpec((CMP_HIDDEN, LANES), lambda i: (0, 0))],
        out_specs=pl.BlockSpec((tm, LANES), lambda i: (i, 0)),
        compiler_params=_params(("parallel",)),
        name='nsa_compress',
    )(x, pos_emb.reshape(1, k), w1.astype(BF16), w2p)
    return out[:, :d].reshape(b, g, n_chunk, d)


def _masked_softmax_rows(s, mask):
    s = jnp.where(mask, s, NEG_INF)
    m = jnp.max(s, axis=-1, keepdims=True)
    p = jnp.where(mask, jnp.exp(s - m), 0.0)
    return p / jnp.maximum(jnp.sum(p, axis=-1, keepdims=True), TINY)


def _nsa_body(q_ref, gl_ref, kct_ref, vc_ref, kst_ref, vs_ref, kwt_ref, vw_ref, o_ref, *,
              tq, tk, n_blocks, n_top, span):
    hg = NSA_HPG
    d = NSA_HEAD_DIM
    rows = hg * tq
    qi = pl.program_id(2)
    q0 = qi * tq
    qf = q_ref[...] * (d ** -0.5)
    q = jnp.concatenate([qf[:, h * d:(h + 1) * d] for h in range(hg)], axis=0).astype(BF16)
    t_q = q0 + lax.broadcasted_iota(jnp.int32, (tq, 1), 0)
    t = jnp.concatenate([t_q] * hg, axis=0)

    n_cmp = kct_ref.shape[-1]
    s_c = jnp.dot(q, kct_ref[...], preferred_element_type=F32)
    cmp_end = lax.broadcasted_iota(jnp.int32, (1, n_cmp), 1) * CMP_STRIDE + (CMP_LEN - 1)
    p_c = _masked_softmax_rows(s_c, cmp_end <= t)
    o_c = jnp.dot(p_c.astype(BF16), vc_ref[...], preferred_element_type=F32)

    p_sum = p_c[0:tq]
    for h in range(1, hg):
        p_sum = p_sum + p_c[h * tq:(h + 1) * tq]
    jj = lax.broadcasted_iota(jnp.int32, (n_cmp, LANES), 0)
    nn = lax.broadcasted_iota(jnp.int32, (n_cmp, LANES), 1)
    per = SLC_LEN // CMP_STRIDE
    pool = 0.5 * ((jj // per == nn).astype(F32) + ((jj + 1) // per == nn).astype(F32))
    imp = _dot_rhs01(p_sum, pool)
    blk = lax.broadcasted_iota(jnp.int32, (tq, LANES), 1)
    cur = t_q // SLC_LEN
    forced = (blk == 0) | (blk == cur) | (blk == cur - 1)
    imp = jnp.where(forced, FORCE_SCORE, imp)
    imp = jnp.where(blk <= cur, imp, -1.0)
    imp = jnp.where(blk < n_blocks, imp, -3.0)
    sel = jnp.zeros((tq, LANES), F32)
    blk_f = blk.astype(F32)
    for _ in range(n_top):
        top = jnp.max(imp, axis=-1, keepdims=True)
        first = jnp.min(jnp.where(imp == top, blk_f, float(LANES)), axis=-1, keepdims=True)
        hit = blk_f == first
        sel = jnp.where(hit, 1.0, sel)
        imp = jnp.where(hit, -2.0, imp)
    sel = sel.astype(BF16)

    def sel_step(kt, carry):
        m, l, acc = carry
        ks = pl.multiple_of(kt * tk, tk)
        s = jnp.dot(q, kst_ref[:, pl.ds(ks, tk)], preferred_element_type=F32)
        bb = lax.broadcasted_iota(jnp.int32, (LANES, tk), 0)
        kk = lax.broadcasted_iota(jnp.int32, (LANES, tk), 1)
        expand = (bb == (ks + kk) // SLC_LEN).astype(BF16)
        chosen = jnp.dot(sel, expand, preferred_element_type=F32)
        chosen = jnp.concatenate([chosen] * hg, axis=0)
        kpos = ks + lax.broadcasted_iota(jnp.int32, (1, tk), 1)
        mask = (chosen > 0.5) & (kpos <= t)
        s = jnp.where(mask, s, NEG_INF)
        m_new = jnp.maximum(m, jnp.max(s, axis=-1, keepdims=True))
        alpha = jnp.exp(m - m_new)
        p = jnp.where(mask, jnp.exp(s - m_new), 0.0)
        l = alpha * l + jnp.sum(p, axis=-1, keepdims=True)
        acc = alpha * acc + jnp.dot(p.astype(BF16), vs_ref[pl.ds(ks, tk), :],
                                    preferred_element_type=F32)
        return m_new, l, acc

    init = (jnp.full((rows, 1), NEG_INF, F32), jnp.zeros((rows, 1), F32), jnp.zeros((rows, d), F32))
    n_kt = (q0 + tq + tk - 1) // tk
    _, l_s, acc_s = lax.fori_loop(0, n_kt, sel_step, init)
    o_s = acc_s / jnp.maximum(l_s, TINY)

    ws = pl.multiple_of(jnp.maximum(q0 + tq - span, 0), LANES)
    s_w = jnp.dot(q, kwt_ref[:, pl.ds(ws, span)], preferred_element_type=F32)
    dist = t - (ws + lax.broadcasted_iota(jnp.int32, (1, span), 1))
    p_w = _masked_softmax_rows(s_w, (dist >= 0) & (dist < WIN))
    o_w = jnp.dot(p_w.astype(BF16), vw_ref[pl.ds(ws, span), :], preferred_element_type=F32)

    gates = _sigmoid(gl_ref[...])
    g_t = []
    for r in range(tq // LANES):
        blk_g = jnp.concatenate([gates[:, r * LANES:(r + 1) * LANES],
                                 jnp.zeros((LANES - gates.shape[0], LANES), F32)], axis=0)
        g_t.append(blk_g.T)
    g_t = jnp.concatenate(g_t, axis=0)

    def gate_col(branch):
        return jnp.concatenate([g_t[:, branch * hg + h:branch * hg + h + 1] for h in range(hg)], axis=0)

    o = gate_col(0) * o_c + gate_col(1) * o_s + gate_col(2) * o_w
    o_ref[...] = jnp.concatenate([o[h * tq:(h + 1) * tq] for h in range(hg)], axis=1).astype(o_ref.dtype)


def _nsa_attention(proj, gl_t, kct, vc, kst, vs, kwt, vw, *, tq=128, tk=512):
    b, s, _ = proj.shape
    g = NSA_GROUPS
    d = NSA_HEAD_DIM
    n_cmp = kct.shape[-1]
    tq = min(tq, s)
    tk = min(tk, s)
    span = min(WIN + tq, s)
    n_blocks = s // SLC_LEN
    assert s % tq == 0 and s % tk == 0 and tq % LANES == 0 and n_blocks <= LANES
    kv_t = lambda n: pl.BlockSpec((None, None, d, n), lambda bi, gi, qi: (bi, gi, 0, 0))
    kv_r = lambda n: pl.BlockSpec((None, None, n, d), lambda bi, gi, qi: (bi, gi, 0, 0))
    return pl.pallas_call(
        functools.partial(_nsa_body, tq=tq, tk=tk, n_blocks=n_blocks,
                          n_top=min(SLC_TOPN, n_blocks), span=span),
        out_shape=jax.ShapeDtypeStruct((b, s, g * NSA_HPG * d), BF16),
        grid=(b, g, s // tq),
        in_specs=[pl.BlockSpec((None, tq, NSA_HPG * d), lambda bi, gi, qi: (bi, qi, gi)),
                  pl.BlockSpec((None, None, 16, tq), lambda bi, gi, qi: (bi, gi, 0, qi)),
                  kv_t(n_cmp), kv_r(n_cmp), kv_t(s), kv_r(s), kv_t(s), kv_r(s)],
        out_specs=pl.BlockSpec((None, tq, NSA_HPG * d), lambda bi, gi, qi: (bi, qi, gi)),
        compiler_params=_params(("parallel", "parallel", "arbitrary")),
        name='nsa_attention',
    )(proj, gl_t, kct, vc, kst, vs, kwt, vw)


def _pad_cols(w, n):
    return jnp.pad(w, ((0, 0), (0, n - w.shape[1])))


def _even_token_mixer(x, mod, norm_g, cos, sin, w_in, conv_w, a_log, dt_bias, dn_norm, q_norm,
                      kv_norm, w_uq, w_ukv, w_out, b, s):
    shift, scale, gate = mod
    n = b * s
    hd = DN_HEADS
    o_b = 4 * DN_W
    o_a = o_b + hd
    o_cq = o_a + hd
    o_ckv = o_cq + MLA_Q_RANK
    o_kr = o_ckv + MLA_KV_RANK
    w_cols = [w_in[:, :o_b], w_in[:, o_cq:o_ckv], w_in[:, o_ckv:o_kr],
              _pad_cols(w_in[:, o_kr:o_kr + MLA_ROPE], LANES),
              jnp.repeat(w_in[:, o_b:o_a], DN_HEAD_DIM, axis=1),
              jnp.repeat(w_in[:, o_a:o_cq], DN_HEAD_DIM, axis=1)]
    w_all = jnp.concatenate(w_cols, axis=1)
    proj = _mm(x, w_all, pro='adaln', pro_args=(norm_g, scale, shift), rows_per_batch=s,
               tn_cap=1024, name='even_in_proj')
    width = proj.shape[1]
    c0 = 4 * DN_W
    c_q = proj[:, c0:c0 + MLA_Q_RANK]
    c_kv = proj[:, c0 + MLA_Q_RANK:c0 + MLA_Q_RANK + MLA_KV_RANK]
    k_r = proj[:, c0 + MLA_Q_RANK + MLA_KV_RANK:c0 + MLA_Q_RANK + MLA_KV_RANK + MLA_ROPE]
    pb_blk = (c0 + MLA_Q_RANK + MLA_KV_RANK + LANES) // LANES
    proj3 = proj.reshape(b, s, width)

    qkv = proj3[:, :, :3 * DN_W]
    pad = jnp.pad(qkv, ((0, 0), (DN_CONV - 1, 0), (0, 0)))
    conv = sum(pad[:, i:i + s, :] * conv_w[i][None, None, :] for i in range(DN_CONV))
    qkv = jax.nn.silu(conv)
    o_dn = _deltanet(qkv, proj3, a_log, dt_bias, dn_norm, z_blk=3 * hd, pb_blk=pb_blk,
                     pa_blk=pb_blk + hd)

    hm = MLA_HEADS
    q = _mm(c_q, w_uq, pro='rms', pro_args=(q_norm,), name='mla_q_up')
    kv = _mm(c_kv, w_ukv, pro='rms', pro_args=(kv_norm,), name='mla_kv_up')
    q = q.reshape(b, s, hm, MLA_QK)
    kv = kv.reshape(b, s, hm, MLA_NOPE + MLA_V)

    def rope(t, c_, s_):
        half = t.shape[-1] // 2
        t1, t2 = t[..., :half], t[..., half:]
        return jnp.concatenate([t1 * c_ - t2 * s_, t2 * c_ + t1 * s_], axis=-1)

    q_rope = rope(q[..., MLA_NOPE:], cos[:, :, None, :], sin[:, :, None, :])
    k_rope = rope(k_r.reshape(b, s, MLA_ROPE), cos, sin)
    qc = jnp.concatenate([q[..., :MLA_NOPE], q_rope], axis=-1) * (MLA_QK ** -0.5)
    kc = jnp.concatenate([kv[..., :MLA_NOPE],
                          jnp.broadcast_to(k_rope[:, :, None, :], (b, s, hm, MLA_ROPE))], axis=-1)
    o_mla = _mla_attention(qc.transpose(0, 2, 1, 3).astype(BF16),
                           kc.transpose(0, 2, 1, 3).astype(BF16),
                           kv[..., MLA_NOPE:].transpose(0, 2, 1, 3).astype(BF16))
    mix = jnp.concatenate([o_dn, o_mla], axis=-1).reshape(n, -1)
    return _mm(mix, w_out, epi='residual', epi_args=(x, gate), rows_per_batch=s, name='even_out_proj')


def _odd_token_mixer(x, mod, norm_g, w_in, pos_k, pos_v, ck1, ck2, cv1, cv2, w_out, b, s):
    shift, scale, gate = mod
    n = b * s
    g = NSA_GROUPS
    d = NSA_HEAD_DIM
    n_in = NSA_Q_W + 6 * NSA_KV_W + 3 * NSA_HEADS
    w_all = _pad_cols(w_in, -(-n_in // LANES) * LANES)
    proj = _mm(x, w_all, pro='adaln', pro_args=(norm_g, scale, shift), rows_per_batch=s,
               tn_cap=1024, name='odd_in_proj')
    proj3 = proj.reshape(b, s, -1)

    def kv(i):
        lo = NSA_Q_W + i * NSA_KV_W
        return proj3[:, :, lo:lo + NSA_KV_W].reshape(b, s, g, d)

    k_cmp = _compress(kv(0), pos_k, ck1, ck2)
    v_cmp = _compress(kv(1), pos_v, cv1, cv2)
    to_t = lambda t: t.transpose(0, 2, 3, 1).astype(BF16)
    to_r = lambda t: t.transpose(0, 2, 1, 3).astype(BF16)
    gl = proj3[:, :, NSA_Q_W + 6 * NSA_KV_W:n_in].reshape(b, s, g, NSA_HPG, 3)
    gl_t = gl.transpose(0, 2, 4, 3, 1).reshape(b, g, 3 * NSA_HPG, s)
    gl_t = jnp.pad(gl_t, ((0, 0), (0, 0), (0, 16 - 3 * NSA_HPG), (0, 0)))
    o = _nsa_attention(proj3, gl_t, k_cmp.transpose(0, 1, 3, 2).astype(BF16), v_cmp.astype(BF16),
                       to_t(kv(2)), to_r(kv(3)), to_t(kv(4)), to_r(kv(5)))
    return _mm(o.reshape(n, -1), w_out, epi='residual', epi_args=(x, gate), rows_per_batch=s,
               name='odd_out_proj')


def _moe(x, mod, norm_g, w_router, b_router, w1, w3, w2, b, s):
    shift, scale, gate = mod
    n = b * s
    h, logits = _norm_router(x, norm_g, scale, shift, w_router, s)
    logits = logits[:, :N_EXPERTS] + b_router.astype(F32)
    top_val, top_idx = lax.top_k(logits, TOP_K)
    gate_w = jax.nn.softmax(top_val, axis=-1)
    flat_e = top_idx.reshape(-1)
    onehot = (flat_e[:, None] == jnp.arange(N_EXPERTS, dtype=jnp.int32)[None, :]).astype(jnp.int32)
    rank = jnp.take_along_axis(jnp.cumsum(onehot, axis=0), flat_e[:, None], axis=1)[:, 0] - 1
    counts = jnp.sum(onehot, axis=0)
    padded = ((counts + MOE_BLOCK - 1) // MOE_BLOCK) * MOE_BLOCK
    pad_end = jnp.cumsum(padded)
    pad_start = pad_end - padded
    dest = pad_start[flat_e] + rank
    n_assign = n * TOP_K
    n_blk = -(-n_assign // MOE_BLOCK) + N_EXPERTS
    n_rows = n_blk * MOE_BLOCK
    flat_tok = jnp.repeat(jnp.arange(n, dtype=jnp.int32), TOP_K)
    row_tok = jnp.zeros((n_rows,), jnp.int32).at[dest].set(flat_tok, unique_indices=True)
    blk_exp = jnp.minimum(jnp.searchsorted(pad_end, jnp.arange(n_blk, dtype=jnp.int32) * MOE_BLOCK,
                                           side='right'), N_EXPERTS - 1).astype(jnp.int32)
    n_used = (pad_end[-1:] // MOE_BLOCK).astype(jnp.int32)
    y_rows = _swiglu_grouped(h[row_tok], blk_exp, n_used, w1, w3, w2)
    y = jnp.sum(y_rows[dest].reshape(n, TOP_K, -1) * gate_w[:, :, None], axis=1)
    return x + jnp.broadcast_to(gate, (b, s, gate.shape[-1])).reshape(n, -1) * y


def kernel(x, c, positions, ada_w, ada_b, norm_g, final_g, ev_w_in, ev_conv_w, ev_a_log, ev_dt_bias, ev_dn_norm, ev_q_norm, ev_kv_norm, ev_w_uq, ev_w_ukv, ev_w_out, ev_ff_gate, ev_ff_up, ev_ff_down, od_w_in, od_cmp_pos_k, od_cmp_pos_v, od_cmp_k1, od_cmp_k2, od_cmp_v1, od_cmp_v2, od_w_out, od_router, od_router_b, od_moe_w1, od_moe_w3, od_moe_w2):
    b, s, dm = x.shape
    depth = ada_w.shape[0]
    inv = 1.0 / (ROPE_THETA ** (jnp.arange(0, MLA_ROPE, 2, dtype=F32) / MLA_ROPE))
    ang = positions.astype(F32)[..., None] * inv
    cos, sin = jnp.cos(ang), jnp.sin(ang)
    mods = jnp.einsum('bd,lkde->lkbe', jax.nn.silu(c), ada_w,
                      precision=lax.Precision.HIGHEST) + ada_b[:, :, None, :]

    def mod(layer, k):
        m = mods[layer, k]
        return tuple(m[:, None, i * dm:(i + 1) * dm] for i in range(3))

    xf = x.reshape(b * s, dm)
    for layer in range(depth):
        j = layer // 2
        if layer % 2 == 0:
            xf = _even_token_mixer(xf, mod(layer, 0), norm_g[layer, 0], cos, sin, ev_w_in[j],
                                   ev_conv_w[j], ev_a_log[j], ev_dt_bias[j], ev_dn_norm[j],
                                   ev_q_norm[j], ev_kv_norm[j], ev_w_uq[j], ev_w_ukv[j],
                                   ev_w_out[j], b, s)
            shift, scale, gate = mod(layer, 1)
            xf = _swiglu_dense(xf, norm_g[layer, 1], scale, shift, ev_ff_gate[j], ev_ff_up[j],
                               ev_ff_down[j], gate, s)
        else:
            xf = _odd_token_mixer(xf, mod(layer, 0), norm_g[layer, 0], od_w_in[j],
                                  od_cmp_pos_k[j], od_cmp_pos_v[j], od_cmp_k1[j], od_cmp_k2[j],
                                  od_cmp_v1[j], od_cmp_v2[j], od_w_out[j], b, s)
            xf = _moe(xf, mod(layer, 1), norm_g[layer, 1], od_router[j], od_router_b[j],
                      od_moe_w1[j], od_moe_w3[j], od_moe_w2[j], b, s)
    return _final_norm(xf, final_g).reshape(b, s, dm)
```

```python
import functools
import math

import jax
import jax.numpy as jnp
import numpy as np
from jax import lax
from jax.experimental import pallas as pl
from jax.experimental.pallas import tpu as pltpu

F32 = jnp.float32
BF16 = jnp.bfloat16

NORM_EPS = 1e-6
NEG_INF = -1e30
TINY = 1e-30

LANES = 128

DN_HEADS = 4
DN_HEAD_DIM = 128
DN_CONV = 4
DN_CHUNK = 64
DN_W = DN_HEADS * DN_HEAD_DIM

MLA_HEADS = 4
MLA_Q_RANK = 256
MLA_KV_RANK = 128
MLA_NOPE = 128
MLA_ROPE = 64
MLA_V = 128
MLA_QK = MLA_NOPE + MLA_ROPE
ROPE_THETA = 10000.0

NSA_HEADS = 16
NSA_GROUPS = 4
NSA_HPG = NSA_HEADS // NSA_GROUPS
NSA_HEAD_DIM = 64
NSA_Q_W = NSA_HEADS * NSA_HEAD_DIM
NSA_KV_W = NSA_GROUPS * NSA_HEAD_DIM
CMP_LEN = 32
CMP_STRIDE = 16
CMP_HIDDEN = 256
SLC_LEN = 64
SLC_TOPN = 16
WIN = 512
FORCE_SCORE = 1e9

N_EXPERTS = 8
TOP_K = 2
MOE_BLOCK = 512

VMEM_LIMIT = 56 * 1024 * 1024


def _params(sem):
    return pltpu.CompilerParams(dimension_semantics=sem, vmem_limit_bytes=VMEM_LIMIT)


def _sigmoid(x):
    return 1.0 / (1.0 + jnp.exp(-x))


def _silu(x):
    return x * _sigmoid(x)


def _dot(a, b):
    return jnp.dot(a.astype(BF16), b.astype(BF16), preferred_element_type=F32)


def _dot_nt(a, b):
    return lax.dot_general(a.astype(BF16), b.astype(BF16), (((1,), (1,)), ((), ())),
                           preferred_element_type=F32)


def _dot_tn(a, b):
    return lax.dot_general(a.astype(BF16), b.astype(BF16), (((0,), (0,)), ((), ())),
                           preferred_element_type=F32)


def _split3(x):
    hi = x.astype(BF16)
    r1 = x - hi.astype(F32)
    mid = r1.astype(BF16)
    lo = (r1 - mid.astype(F32)).astype(BF16)
    return hi, mid, lo


def _dot_lhs01(a01, x):
    hi, mid, lo = _split3(x)
    a = a01.astype(BF16)
    d = functools.partial(jnp.dot, preferred_element_type=F32)
    return d(a, hi) + d(a, mid) + d(a, lo)


def _dot_rhs01(x, b01):
    hi, mid, lo = _split3(x)
    b = b01.astype(BF16)
    d = functools.partial(jnp.dot, preferred_element_type=F32)
    return d(hi, b) + d(mid, b) + d(lo, b)


def _dot_hi(a, b):
    ah = a.astype(BF16)
    al = (a - ah.astype(F32)).astype(BF16)
    bh = b.astype(BF16)
    bl = (b - bh.astype(F32)).astype(BF16)
    d = functools.partial(jnp.dot, preferred_element_type=F32)
    return d(ah, bh) + d(ah, bl) + d(al, bh)


def _rms(x, gain):
    return x * lax.rsqrt(jnp.mean(x * x, axis=-1, keepdims=True) + NORM_EPS) * gain


def _mm_body(*refs, pro, epi):
    x_ref, w_ref = refs[0], refs[1]
    pos = 2
    if pro == 'adaln':
        g_ref, sc_ref, sh_ref = refs[pos:pos + 3]
        pos += 3
    elif pro == 'rms':
        g_ref = refs[pos]
        pos += 1
    if epi == 'residual':
        res_ref, gate_ref = refs[pos:pos + 2]
        pos += 2
    o_ref = refs[pos]
    pos += 1
    if pro is not None:
        h_ref = refs[pos]

        @pl.when(pl.program_id(1) == 0)
        def _():
            y = _rms(x_ref[...].astype(F32), g_ref[...])
            if pro == 'adaln':
                y = y * (1.0 + sc_ref[...]) + sh_ref[...]
            h_ref[...] = y.astype(BF16)

        h = h_ref[...]
    else:
        h = x_ref[...].astype(BF16)
    acc = jnp.dot(h, w_ref[...], preferred_element_type=F32)
    if epi == 'residual':
        acc = res_ref[...] + gate_ref[...] * acc
    o_ref[...] = acc.astype(o_ref.dtype)


def _pick_tile(n, cap):
    best = LANES
    t = LANES
    while t <= min(n, cap):
        if n % t == 0:
            best = t
        t += LANES
    return best


def _mm(x, w, *, pro=None, pro_args=(), epi=None, epi_args=(), rows_per_batch=None,
        out_dtype=F32, tm=512, tn_cap=1024, name='mm'):
    m, k = x.shape
    n = w.shape[1]
    tm = min(tm, m)
    tn = _pick_tile(n, tn_cap)
    assert m % tm == 0 and n % tn == 0
    rpb = None if rows_per_batch is None else rows_per_batch // tm
    in_specs = [pl.BlockSpec((tm, k), lambda i, j: (i, 0)),
                pl.BlockSpec((k, tn), lambda i, j: (0, j))]
    args = [x, w.astype(BF16)]
    scratch = []
    if pro == 'adaln':
        g, sc, sh = pro_args
        in_specs += [pl.BlockSpec((1, k), lambda i, j: (0, 0)),
                     pl.BlockSpec((None, 1, k), lambda i, j: (i // rpb, 0, 0)),
                     pl.BlockSpec((None, 1, k), lambda i, j: (i // rpb, 0, 0))]
        args += [g.reshape(1, k), sc, sh]
    elif pro == 'rms':
        in_specs += [pl.BlockSpec((1, k), lambda i, j: (0, 0))]
        args += [pro_args[0].reshape(1, k)]
    if pro is not None:
        scratch = [pltpu.VMEM((tm, k), BF16)]
    if epi == 'residual':
        res, gate = epi_args
        in_specs += [pl.BlockSpec((tm, tn), lambda i, j: (i, j)),
                     pl.BlockSpec((None, 1, tn), lambda i, j: (i // rpb, 0, j))]
        args += [res, gate]
    return pl.pallas_call(
        functools.partial(_mm_body, pro=pro, epi=epi),
        out_shape=jax.ShapeDtypeStruct((m, n), out_dtype),
        grid=(m // tm, n // tn),
        in_specs=in_specs,
        out_specs=pl.BlockSpec((tm, tn), lambda i, j: (i, j)),
        scratch_shapes=scratch,
        compiler_params=_params(("parallel", "arbitrary")),
        name=name,
    )(*args)


def _swiglu_dense_body(x_ref, g_ref, sc_ref, sh_ref, wg_ref, wu_ref, wd_ref, gate_ref, o_ref,
                       h_ref, acc_ref):
    j = pl.program_id(1)

    @pl.when(j == 0)
    def _():
        y = _rms(x_ref[...], g_ref[...]) * (1.0 + sc_ref[...]) + sh_ref[...]
        h_ref[...] = y.astype(BF16)
        acc_ref[...] = jnp.zeros_like(acc_ref)

    h = h_ref[...]
    a = jnp.dot(h, wg_ref[...], preferred_element_type=F32)
    b = jnp.dot(h, wu_ref[...], preferred_element_type=F32)
    act = (_silu(a) * b).astype(BF16)
    acc_ref[...] += jnp.dot(act, wd_ref[...], preferred_element_type=F32)

    @pl.when(j == pl.num_programs(1) - 1)
    def _():
        o_ref[...] = x_ref[...] + gate_ref[...] * acc_ref[...]


def _swiglu_dense(x, g, sc, sh, wg, wu, wd, gate, rows_per_batch, *, tm=1024, tf=256):
    m, k = x.shape
    f = wg.shape[1]
    assert m % tm == 0 and f % tf == 0
    rpb = rows_per_batch // tm
    return pl.pallas_call(
        _swiglu_dense_body,
        out_shape=jax.ShapeDtypeStruct((m, k), F32),
        grid=(m // tm, f // tf),
        in_specs=[pl.BlockSpec((tm, k), lambda i, j: (i, 0)),
                  pl.BlockSpec((1, k), lambda i, j: (0, 0)),
                  pl.BlockSpec((None, 1, k), lambda i, j: (i // rpb, 0, 0)),
                  pl.BlockSpec((None, 1, k), lambda i, j: (i // rpb, 0, 0)),
                  pl.BlockSpec((k, tf), lambda i, j: (0, j)),
                  pl.BlockSpec((k, tf), lambda i, j: (0, j)),
                  pl.BlockSpec((tf, k), lambda i, j: (j, 0)),
                  pl.BlockSpec((None, 1, k), lambda i, j: (i // rpb, 0, 0))],
        out_specs=pl.BlockSpec((tm, k), lambda i, j: (i, 0)),
        scratch_shapes=[pltpu.VMEM((tm, k), BF16), pltpu.VMEM((tm, k), F32)],
        compiler_params=_params(("parallel", "arbitrary")),
        name='swiglu_dense',
    )(x, g.reshape(1, k), sc, sh, wg.astype(BF16), wu.astype(BF16), wd.astype(BF16), gate)


def _swiglu_grouped_body(be_ref, nu_ref, x_ref, rw_ref, wg_ref, wu_ref, wd_ref, o_ref, acc_ref):
    i = pl.program_id(0)
    j = pl.program_id(1)
    used = i < nu_ref[0]

    @pl.when(j == 0)
    def _():
        acc_ref[...] = jnp.zeros_like(acc_ref)

    @pl.when(used)
    def _():
        h = x_ref[...]
        a = jnp.dot(h, wg_ref[...], preferred_element_type=F32)
        b = jnp.dot(h, wu_ref[...], preferred_element_type=F32)
        act = (_silu(a) * b).astype(BF16)
        acc_ref[...] += jnp.dot(act, wd_ref[...], preferred_element_type=F32)

    @pl.when(j == pl.num_programs(1) - 1)
    def _():
        o_ref[...] = (acc_ref[...] * rw_ref[...]).astype(o_ref.dtype)


def _swiglu_grouped(x_rows, row_w, blk_exp, n_used, w1, w3, w2, *, tm=MOE_BLOCK, tf=512):
    n_rows, k = x_rows.shape
    f = w1.shape[2]
    assert n_rows % tm == 0 and f % tf == 0
    nj = f // tf

    def jj(i, j, nu):
        return jnp.where(i < nu[0], j, nj - 1)

    grid_spec = pltpu.PrefetchScalarGridSpec(
        num_scalar_prefetch=2,
        grid=(n_rows // tm, nj),
        in_specs=[pl.BlockSpec((tm, k), lambda i, j, be, nu: (i, 0)),
                  pl.BlockSpec((tm, 1), lambda i, j, be, nu: (i, 0)),
                  pl.BlockSpec((None, k, tf), lambda i, j, be, nu: (be[i], 0, jj(i, j, nu))),
                  pl.BlockSpec((None, k, tf), lambda i, j, be, nu: (be[i], 0, jj(i, j, nu))),
                  pl.BlockSpec((None, tf, k), lambda i, j, be, nu: (be[i], jj(i, j, nu), 0))],
        out_specs=pl.BlockSpec((tm, k), lambda i, j, be, nu: (i, 0)),
        scratch_shapes=[pltpu.VMEM((tm, k), F32)],
    )
    return pl.pallas_call(
        _swiglu_grouped_body,
        out_shape=jax.ShapeDtypeStruct((n_rows, k), BF16),
        grid_spec=grid_spec,
        compiler_params=_params(("arbitrary", "arbitrary")),
        name='swiglu_grouped',
    )(blk_exp, n_used, x_rows, row_w, w1.astype(BF16), w3.astype(BF16), w2.astype(BF16))


def _norm_router_body(x_ref, g_ref, sc_ref, sh_ref, wr_ref, h_ref, lg_ref):
    y = _rms(x_ref[...], g_ref[...]) * (1.0 + sc_ref[...]) + sh_ref[...]
    h_ref[...] = y.astype(BF16)
    lg_ref[...] = jnp.dot(y, wr_ref[...], preferred_element_type=F32,
                          precision=lax.Precision.HIGHEST)


def _norm_router(x, g, sc, sh, w_router, rows_per_batch, *, tm=512):
    m, k = x.shape
    e = w_router.shape[1]
    wr = jnp.zeros((k, LANES), F32).at[:, :e].set(w_router)
    rpb = rows_per_batch // tm
    return pl.pallas_call(
        _norm_router_body,
        out_shape=(jax.ShapeDtypeStruct((m, k), BF16), jax.ShapeDtypeStruct((m, LANES), F32)),
        grid=(m // tm,),
        in_specs=[pl.BlockSpec((tm, k), lambda i: (i, 0)),
                  pl.BlockSpec((1, k), lambda i: (0, 0)),
                  pl.BlockSpec((None, 1, k), lambda i: (i // rpb, 0, 0)),
                  pl.BlockSpec((None, 1, k), lambda i: (i // rpb, 0, 0)),
                  pl.BlockSpec((k, LANES), lambda i: (0, 0))],
        out_specs=(pl.BlockSpec((tm, k), lambda i: (i, 0)),
                   pl.BlockSpec((tm, LANES), lambda i: (i, 0))),
        compiler_params=_params(("parallel",)),
        name='norm_router',
    )(x, g.reshape(1, k), sc, sh, wr)


def _final_norm_body(x_ref, g_ref, o_ref):
    o_ref[...] = _rms(x_ref[...], g_ref[...])


def _final_norm(x, g, *, tm=1024):
    m, k = x.shape
    return pl.pallas_call(
        _final_norm_body,
        out_shape=jax.ShapeDtypeStruct((m, k), F32),
        grid=(m // tm,),
        in_specs=[pl.BlockSpec((tm, k), lambda i: (i, 0)), pl.BlockSpec((1, k), lambda i: (0, 0))],
        out_specs=pl.BlockSpec((tm, k), lambda i: (i, 0)),
        compiler_params=_params(("parallel",)),
        name='final_norm',
    )(x, g.reshape(1, k))


def _softplus(x):
    return jnp.maximum(x, 0.0) + jnp.log(1.0 + jnp.exp(-jnp.abs(x)))


CONV_HALO = 8


def _deltanet_body(q_ref, k_ref, v_ref, z_ref, pb_ref, pa_ref, cwq_ref, cwk_ref, cwv_ref, alog_ref,
                   dtb_ref, gn_ref, o_ref, s_ref, xs_ref, *, n_chunks):
    c_len = DN_CHUNK
    tile = n_chunks * c_len

    @pl.when(pl.program_id(1) == 0)
    def _():
        s_ref[...] = jnp.zeros_like(s_ref)
        xs_ref[:, 0:CONV_HALO, :] = jnp.zeros((3, CONV_HALO, xs_ref.shape[-1]), F32)

    def conv(i, x_ref, cw_ref):
        xs_ref[i, CONV_HALO:CONV_HALO + tile, :] = x_ref[...]
        y = None
        for tap in range(DN_CONV):
            off = CONV_HALO - (DN_CONV - 1) + tap
            term = xs_ref[i, off:off + tile, :] * cw_ref[tap:tap + 1, :]
            y = term if y is None else y + term
        xs_ref[i, 0:CONV_HALO, :] = xs_ref[i, tile:tile + CONV_HALO, :]
        return _silu(y)

    q = conv(0, q_ref, cwq_ref)
    k = conv(1, k_ref, cwk_ref)
    v = conv(2, v_ref, cwv_ref)

    row = lax.broadcasted_iota(jnp.int32, (c_len, c_len), 0)
    col = lax.broadcasted_iota(jnp.int32, (c_len, c_len), 1)
    incl = row >= col
    strict = row > col
    incl_f = incl.astype(F32)
    strict_f = strict.astype(F32)
    eye = (row == col).astype(F32)
    q = q * lax.rsqrt(jnp.sum(q * q, axis=-1, keepdims=True) + NORM_EPS) * (DN_HEAD_DIM ** -0.5)
    k = k * lax.rsqrt(jnp.sum(k * k, axis=-1, keepdims=True) + NORM_EPS)
    beta = _sigmoid(pb_ref[...])
    g = -jnp.exp(alog_ref[...]) * _softplus(pa_ref[...] + dtb_ref[...])
    kb = k * beta
    vb = v * beta
    chunks = range(n_chunks)
    cs = lambda x, c: x[c * c_len:(c + 1) * c_len]
    gc = [_dot_lhs01(incl_f, cs(g, c)) for c in chunks]
    gdiff = [_dot_lhs01(incl_f, cs(g, c)[:, :c_len] * strict_f) for c in chunks]
    decay = [jnp.where(incl, jnp.exp(jnp.where(incl, gd, 0.0)), 0.0) for gd in gdiff]
    eg = [jnp.exp(x) for x in gc]
    lower = [jnp.where(strict, _dot_nt(cs(kb, c), cs(k, c)) * decay[c], 0.0) for c in chunks]
    tinv = [eye - lo for lo in lower]
    pw = lower
    for it in range(5):
        mul = _dot_hi if it == 0 else _dot
        pw = [mul(p, p) for p in pw]
        tinv = [t + mul(t, p) for t, p in zip(tinv, pw)]
    u = [_dot(tinv[c], cs(vb, c)) for c in chunks]
    w = [_dot(tinv[c], cs(kb, c) * eg[c]) for c in chunks]
    qk = [_dot_nt(cs(q, c), cs(k, c)) * decay[c] for c in chunks]
    g_last = [x[c_len - 1:c_len, :] for x in gc]
    k_tail = [cs(k, c) * jnp.exp(g_last[c] - gc[c]) for c in chunks]
    m_mat = [_dot_tn(k_tail[c], w[c]) for c in chunks]
    b_mat = [_dot_tn(k_tail[c], u[c]) for c in chunks]
    q_eff = [cs(q, c) * eg[c] - _dot(qk[c], w[c]) for c in chunks]
    o_loc = [_dot(qk[c], u[c]) for c in chunks]
    state = s_ref[...]
    outs = []
    for c in chunks:
        outs.append(_dot(q_eff[c], state) + o_loc[c])
        state = state * jnp.exp(g_last[c]) - _dot(m_mat[c], state) + b_mat[c]
    s_ref[...] = state
    o = jnp.concatenate(outs, axis=0)
    z = z_ref[...]
    o_ref[...] = (_rms(o, gn_ref[...]) * _silu(z)).astype(o_ref.dtype)


def _deltanet(proj, conv_w, a_log, dt_bias, dn_norm, *, z_blk, pb_blk, pa_blk, tile=512):
    b, s, _ = proj.shape
    h = DN_HEADS
    d = DN_HEAD_DIM
    tile = min(tile, s)
    assert s % tile == 0 and tile % DN_CHUNK == 0
    rep = lambda t: jnp.broadcast_to(t.astype(F32)[:, None, None], (h, 1, d))

    def col(off):
        return pl.BlockSpec((None, tile, d), lambda bh, t, off=off: (bh // h, t, off + bh % h))

    def cw(off):
        return pl.BlockSpec((DN_CONV, d), lambda bh, t, off=off: (0, off + bh % h))

    per_head = pl.BlockSpec((None, 1, d), lambda bh, t: (bh % h, 0, 0))
    return pl.pallas_call(
        functools.partial(_deltanet_body, n_chunks=tile // DN_CHUNK),
        out_shape=jax.ShapeDtypeStruct((b, s, h * d), BF16),
        grid=(b * h, s // tile),
        in_specs=[col(0), col(h), col(2 * h), col(z_blk), col(pb_blk), col(pa_blk),
                  cw(0), cw(h), cw(2 * h),
                  per_head, per_head, pl.BlockSpec((1, d), lambda bh, t: (0, 0))],
        out_specs=col(0),
        scratch_shapes=[pltpu.VMEM((d, d), F32), pltpu.VMEM((3, tile + CONV_HALO, d), F32)],
        compiler_params=_params(("parallel", "arbitrary")),
        name='deltanet',
    )(proj, proj, proj, proj, proj, proj, conv_w, conv_w, conv_w, rep(a_log), rep(dt_bias),
      dn_norm.reshape(1, d))


def _mla_body(q_ref, k_ref, v_ref, o_ref, *, tq, tk):
    qi = pl.program_id(2)
    q = q_ref[...]

    def step(kt, carry, masked):
        m, l, acc = carry
        ks = pl.multiple_of(kt * tk, tk)
        k = k_ref[pl.ds(ks, tk), :]
        v = v_ref[pl.ds(ks, tk), :]
        s = lax.dot_general(q, k, (((1,), (1,)), ((), ())), preferred_element_type=F32)
        if masked:
            tpos = qi * tq + lax.broadcasted_iota(jnp.int32, (tq, tk), 0)
            kpos = ks + lax.broadcasted_iota(jnp.int32, (tq, tk), 1)
            s = jnp.where(kpos <= tpos, s, NEG_INF)
        m_new = jnp.maximum(m, jnp.max(s, axis=-1, keepdims=True))
        alpha = jnp.exp(m - m_new)
        p = jnp.exp(s - m_new)
        l = alpha * l + jnp.sum(p, axis=-1, keepdims=True)
        acc = alpha * acc + jnp.dot(p.astype(BF16), v, preferred_element_type=F32)
        return m_new, l, acc

    init = (jnp.full((tq, 1), NEG_INF, F32), jnp.zeros((tq, 1), F32),
            jnp.zeros((tq, v_ref.shape[-1]), F32))
    n_full = qi * (tq // tk)
    carry = lax.fori_loop(0, n_full, lambda kt, c: step(kt, c, False), init)
    for d in range(tq // tk):
        carry = step(n_full + d, carry, True)
    _, l, acc = carry
    o_ref[...] = (acc / l).astype(o_ref.dtype)


def _mla_attention(q, k, v, *, tq=512, tk=512):
    b, h, s, dqk = q.shape
    dv = v.shape[-1]
    tq = min(tq, s)
    tk = min(tk, tq)
    assert s % tq == 0 and tq % tk == 0
    return pl.pallas_call(
        functools.partial(_mla_body, tq=tq, tk=tk),
        out_shape=jax.ShapeDtypeStruct((b, s, h * dv), BF16),
        grid=(b, h, s // tq),
        in_specs=[pl.BlockSpec((None, None, tq, dqk), lambda bi, hi, qi: (bi, hi, qi, 0)),
                  pl.BlockSpec((None, None, s, dqk), lambda bi, hi, qi: (bi, hi, 0, 0)),
                  pl.BlockSpec((None, None, s, dv), lambda bi, hi, qi: (bi, hi, 0, 0))],
        out_specs=pl.BlockSpec((None, tq, dv), lambda bi, hi, qi: (bi, qi, hi)),
        compiler_params=_params(("parallel", "parallel", "arbitrary")),
        name='mla_attention',
    )(q, k, v)


SUBLANES = 8


def _compress_body(x_ref, pos_ref, w1_ref, w2_ref, o_ref, sh_ref):
    n, half = x_ref.shape
    x = x_ref[...]
    first = jnp.dot((x + pos_ref[0:1, :]).astype(BF16), w1_ref[0:half, :],
                    preferred_element_type=F32)
    second = jnp.dot((x + pos_ref[1:2, :]).astype(BF16), w1_ref[half:2 * half, :],
                     preferred_element_type=F32)
    sh_ref[0:n, :] = second
    sh_ref[n:n + SUBLANES, :] = jnp.zeros((SUBLANES, sh_ref.shape[-1]), F32)
    hid = _silu(first + sh_ref[1:n + 1, :])
    o_ref[...] = jnp.dot(hid.astype(BF16), w2_ref[...], preferred_element_type=F32)


def _compress(t, pos_emb, w1, w2):
    assert CMP_LEN == 2 * CMP_STRIDE
    b, s, g, d = t.shape
    n_chunk = s // CMP_STRIDE
    half = CMP_STRIDE * d
    ch = t.reshape(b, n_chunk, CMP_STRIDE, g, d).transpose(0, 3, 1, 2, 4).reshape(b, g, n_chunk, half)
    w2p = jnp.zeros((CMP_HIDDEN, LANES), BF16).at[:, :d].set(w2.astype(BF16))
    out = pl.pallas_call(
        _compress_body,
        out_shape=jax.ShapeDtypeStruct((b, g, n_chunk, LANES), F32),
        grid=(b, g),
        in_specs=[pl.BlockSpec((None, None, n_chunk, half), lambda bi, gi: (bi, gi, 0, 0)),
                  pl.BlockSpec((2, half), lambda bi, gi: (0, 0)),
                  pl.BlockSpec((2 * half, CMP_HIDDEN), lambda bi, gi: (0, 0)),
                  pl.BlockSpec((CMP_HIDDEN, LANES), lambda bi, gi: (0, 0))],
        out_specs=pl.BlockSpec((None, None, n_chunk, LANES), lambda bi, gi: (bi, gi, 0, 0)),
        scratch_shapes=[pltpu.VMEM((n_chunk + SUBLANES, CMP_HIDDEN), F32)],
        compiler_params=_params(("parallel", "parallel")),
        name='nsa_compress',
    )(ch, pos_emb.reshape(2, half), w1.astype(BF16), w2p)
    return out[..., :d]


LOG2E = 1.4426950408889634
MASK_BIG = 1e30


def _nsa_body(q_ref, gl_ref, kct_ref, vc_ref, kse_ref, vs_ref, kwt_ref, vw_ref, o_ref, *,
              tq, tk, n_blocks, n_top, span):
    hg = NSA_HPG
    d = NSA_HEAD_DIM
    rows = hg * tq
    qi = pl.program_id(2)
    q0 = qi * tq
    qf = q_ref[...] * (d ** -0.5 * LOG2E)
    q = jnp.concatenate([qf[:, h * d:(h + 1) * d] for h in range(hg)], axis=0).astype(BF16)
    t_q = q0 + lax.broadcasted_iota(jnp.int32, (tq, 1), 0)

    def add_bias(s, bias):
        n = s.shape[-1]
        return (s.reshape(hg, tq, n) + bias[None]).reshape(rows, n)


    n_cmp = kct_ref.shape[-1]
    cmp_end = lax.broadcasted_iota(jnp.int32, (1, n_cmp), 1) * CMP_STRIDE + (CMP_LEN - 1)
    bias_c = jnp.where(cmp_end <= t_q, 0.0, NEG_INF)
    s_c = add_bias(jnp.dot(q, kct_ref[...], preferred_element_type=F32), bias_c)
    e_c = jnp.exp2(s_c - jnp.max(s_c, axis=-1, keepdims=True))
    acc_c = jnp.dot(e_c.astype(BF16), vc_ref[...], preferred_element_type=F32)
    has_c = jnp.concatenate([t_q >= CMP_LEN - 1] * hg, axis=0)
    inv_c = jnp.where(has_c, 1.0 / jnp.maximum(acc_c[:, d:d + 1], TINY), 0.0)
    o_c = acc_c[:, :d] * inv_c

    p_sum = jnp.sum((e_c * inv_c).reshape(hg, tq, n_cmp), axis=0)
    jj = lax.broadcasted_iota(jnp.int32, (n_cmp, LANES), 0)
    nn = lax.broadcasted_iota(jnp.int32, (n_cmp, LANES), 1)
    per = SLC_LEN // CMP_STRIDE
    pool = 0.5 * ((jj // per == nn).astype(F32) + ((jj + 1) // per == nn).astype(F32))
    imp = _dot_rhs01(p_sum, pool)
    sel_parts = []
    blk = lax.broadcasted_iota(jnp.int32, (LANES, LANES), 0)
    blk_f = blk.astype(F32)
    for r in range(tq // LANES):
        imp_t = imp[r * LANES:(r + 1) * LANES].T
        cur = (q0 + r * LANES + lax.broadcasted_iota(jnp.int32, (1, LANES), 1)) // SLC_LEN
        forced = (blk == 0) | (blk == cur) | (blk == cur - 1)
        imp_t = jnp.where(forced, FORCE_SCORE, imp_t)
        imp_t = jnp.where(blk <= cur, imp_t, -1.0)
        imp_t = jnp.where(blk < n_blocks, imp_t, -3.0)
        sel_t = jnp.zeros((LANES, LANES), F32)
        for _ in range(n_top):
            top = jnp.max(imp_t, axis=0, keepdims=True)
            first = jnp.min(jnp.where(imp_t == top, blk_f, float(LANES)), axis=0, keepdims=True)
            hit = blk_f == first
            sel_t = jnp.where(hit, 1.0, sel_t)
            imp_t = jnp.where(hit, -2.0, imp_t)
        sel_parts.append(sel_t.T)
    sel = jnp.concatenate(sel_parts, axis=0)
    unsel = jnp.concatenate([sel - 1.0] * hg, axis=0)
    q_aug = jnp.concatenate([unsel.astype(BF16), q], axis=1)
    t_rel = t_q - lax.broadcasted_iota(jnp.int32, (1, tk), 1)

    def sel_tile(kt, carry, causal):
        m, acc = carry
        ks = pl.multiple_of(kt * tk, tk)
        s = jnp.dot(q_aug, kse_ref[:, pl.ds(ks, tk)], preferred_element_type=F32)
        if causal:
            s = add_bias(s, jnp.where(t_rel >= ks, 0.0, NEG_INF))
        m_new = jnp.maximum(m, jnp.max(s, axis=-1, keepdims=True))
        p = jnp.exp2(s - m_new)
        acc = jnp.exp2(m - m_new) * acc + jnp.dot(p.astype(BF16), vs_ref[pl.ds(ks, tk), :],
                                                  preferred_element_type=F32)
        return m_new, acc

    init = (jnp.full((rows, 1), NEG_INF, F32), jnp.zeros((rows, vs_ref.shape[-1]), F32))
    n_kt = (q0 + tq + tk - 1) // tk
    carry = lax.fori_loop(0, n_kt - 1, lambda kt, c: sel_tile(kt, c, False), init)
    _, acc_s = sel_tile(n_kt - 1, carry, True)
    o_s = acc_s[:, :d] / jnp.maximum(acc_s[:, d:d + 1], TINY)

    ws = pl.multiple_of(jnp.maximum(q0 + tq - span, 0), LANES)
    dist = (t_q - ws) - lax.broadcasted_iota(jnp.int32, (1, span), 1)
    bias_w = jnp.where((dist >= 0) & (dist < WIN), 0.0, NEG_INF)
    s_w = add_bias(jnp.dot(q, kwt_ref[:, pl.ds(ws, span)], preferred_element_type=F32), bias_w)
    e_w = jnp.exp2(s_w - jnp.max(s_w, axis=-1, keepdims=True))
    acc_w = jnp.dot(e_w.astype(BF16), vw_ref[pl.ds(ws, span), :], preferred_element_type=F32)
    o_w = acc_w[:, :d] / acc_w[:, d:d + 1]

    gates = _sigmoid(gl_ref[...])
    g_t = []
    for r in range(tq // LANES):
        blk_g = jnp.concatenate([gates[:, r * LANES:(r + 1) * LANES],
                                 jnp.zeros((LANES - gates.shape[0], LANES), F32)], axis=0)
        g_t.append(blk_g.T)
    g_t = jnp.concatenate(g_t, axis=0)

    def gate_col(branch):
        return jnp.concatenate([g_t[:, branch * hg + h:branch * hg + h + 1] for h in range(hg)], axis=0)

    o = gate_col(0) * o_c + gate_col(1) * o_s + gate_col(2) * o_w
    o_ref[...] = jnp.concatenate([o[h * tq:(h + 1) * tq] for h in range(hg)], axis=1).astype(o_ref.dtype)


def _nsa_attention(proj, gl_t, kct, vc, kst, vs, kwt, vw, *, tq=256, tk=512):
    b, s, _ = proj.shape
    g = NSA_GROUPS
    d = NSA_HEAD_DIM
    n_cmp = kct.shape[-1]
    tq = min(tq, s)
    tk = min(tk, s)
    span = min(WIN + tq, s)
    n_blocks = s // SLC_LEN
    assert s % tq == 0 and s % tk == 0 and tq % LANES == 0 and n_blocks <= LANES
    kv_t = lambda n, r=d: pl.BlockSpec((None, None, r, n), lambda bi, gi, qi: (bi, gi, 0, 0))
    kv_r = lambda n: pl.BlockSpec((None, None, n, LANES), lambda bi, gi, qi: (bi, gi, 0, 0))

    def with_ones(v):
        tail = jnp.zeros(v.shape[:-1] + (LANES - d,), BF16).at[..., 0].set(1.0)
        return jnp.concatenate([v, tail], axis=-1)

    blk_of_key = jnp.arange(s, dtype=jnp.int32) // SLC_LEN
    mask_rows = jnp.where(jnp.arange(LANES, dtype=jnp.int32)[:, None] == blk_of_key[None, :],
                          MASK_BIG, 0.0).astype(BF16)
    kse = jnp.concatenate([jnp.broadcast_to(mask_rows, (b, g, LANES, s)), kst], axis=2)
    vc, vs, vw = with_ones(vc), with_ones(vs), with_ones(vw)
    kst = kse
    return pl.pallas_call(
        functools.partial(_nsa_body, tq=tq, tk=tk, n_blocks=n_blocks,
                          n_top=min(SLC_TOPN, n_blocks), span=span),
        out_shape=jax.ShapeDtypeStruct((b, s, g * NSA_HPG * d), BF16),
        grid=(b, g, s // tq),
        in_specs=[pl.BlockSpec((None, tq, NSA_HPG * d), lambda bi, gi, qi: (bi, qi, gi)),
                  pl.BlockSpec((None, None, 16, tq), lambda bi, gi, qi: (bi, gi, 0, qi)),
                  kv_t(n_cmp), kv_r(n_cmp), kv_t(s, LANES + d), kv_r(s), kv_t(s), kv_r(s)],
        out_specs=pl.BlockSpec((None, tq, NSA_HPG * d), lambda bi, gi, qi: (bi, qi, gi)),
        compiler_params=_params(("parallel", "parallel", "arbitrary")),
        name='nsa_attention',
    )(proj, gl_t, kct, vc, kst, vs, kwt, vw)


def _pad_cols(w, n):
    return jnp.pad(w, ((0, 0), (0, n - w.shape[1])))


def _even_token_mixer(x, mod, norm_g, cos, sin, w_in, conv_w, a_log, dt_bias, dn_norm, q_norm,
                      kv_norm, w_uq, w_ukv, w_out, b, s):
    shift, scale, gate = mod
    n = b * s
    hd = DN_HEADS
    o_b = 4 * DN_W
    o_a = o_b + hd
    o_cq = o_a + hd
    o_ckv = o_cq + MLA_Q_RANK
    o_kr = o_ckv + MLA_KV_RANK
    w_cols = [w_in[:, :o_b], w_in[:, o_cq:o_ckv], w_in[:, o_ckv:o_kr],
              _pad_cols(w_in[:, o_kr:o_kr + MLA_ROPE], LANES),
              jnp.repeat(w_in[:, o_b:o_a], DN_HEAD_DIM, axis=1),
              jnp.repeat(w_in[:, o_a:o_cq], DN_HEAD_DIM, axis=1)]
    w_all = jnp.concatenate(w_cols, axis=1)
    proj = _mm(x, w_all, pro='adaln', pro_args=(norm_g, scale, shift), rows_per_batch=s,
               tn_cap=1024, name='even_in_proj')
    width = proj.shape[1]
    c0 = 4 * DN_W
    c_q = proj[:, c0:c0 + MLA_Q_RANK]
    c_kv = proj[:, c0 + MLA_Q_RANK:c0 + MLA_Q_RANK + MLA_KV_RANK]
    k_r = proj[:, c0 + MLA_Q_RANK + MLA_KV_RANK:c0 + MLA_Q_RANK + MLA_KV_RANK + MLA_ROPE]
    pb_blk = (c0 + MLA_Q_RANK + MLA_KV_RANK + LANES) // LANES
    proj3 = proj.reshape(b, s, width)

    o_dn = _deltanet(proj3, conv_w.astype(F32), a_log, dt_bias, dn_norm, z_blk=3 * hd,
                     pb_blk=pb_blk, pa_blk=pb_blk + hd)

    hm = MLA_HEADS
    q = _mm(c_q, w_uq, pro='rms', pro_args=(q_norm,), name='mla_q_up')
    kv = _mm(c_kv, w_ukv, pro='rms', pro_args=(kv_norm,), name='mla_kv_up')
    q = q.reshape(b, s, hm, MLA_QK)
    kv = kv.reshape(b, s, hm, MLA_NOPE + MLA_V)

    def rope(t, c_, s_):
        half = t.shape[-1] // 2
        t1, t2 = t[..., :half], t[..., half:]
        return jnp.concatenate([t1 * c_ - t2 * s_, t2 * c_ + t1 * s_], axis=-1)

    q_rope = rope(q[..., MLA_NOPE:], cos[:, :, None, :], sin[:, :, None, :])
    k_rope = rope(k_r.reshape(b, s, MLA_ROPE), cos, sin)
    qc = jnp.concatenate([q[..., :MLA_NOPE], q_rope], axis=-1) * (MLA_QK ** -0.5)
    kc = jnp.concatenate([kv[..., :MLA_NOPE],
                          jnp.broadcast_to(k_rope[:, :, None, :], (b, s, hm, MLA_ROPE))], axis=-1)
    o_mla = _mla_attention(qc.transpose(0, 2, 1, 3).astype(BF16),
                           kc.transpose(0, 2, 1, 3).astype(BF16),
                           kv[..., MLA_NOPE:].transpose(0, 2, 1, 3).astype(BF16))
    mix = jnp.concatenate([o_dn, o_mla], axis=-1).reshape(n, -1)
    return _mm(mix, w_out, epi='residual', epi_args=(x, gate), rows_per_batch=s, name='even_out_proj')


def _odd_token_mixer(x, mod, norm_g, w_in, pos_k, pos_v, ck1, ck2, cv1, cv2, w_out, b, s):
    shift, scale, gate = mod
    n = b * s
    g = NSA_GROUPS
    d = NSA_HEAD_DIM
    n_in = NSA_Q_W + 6 * NSA_KV_W + 3 * NSA_HEADS
    w_all = _pad_cols(w_in, -(-n_in // LANES) * LANES)
    proj = _mm(x, w_all, pro='adaln', pro_args=(norm_g, scale, shift), rows_per_batch=s,
               tn_cap=1024, name='odd_in_proj')
    proj3 = proj.reshape(b, s, -1)

    def kv(i):
        lo = NSA_Q_W + i * NSA_KV_W
        return proj3[:, :, lo:lo + NSA_KV_W].reshape(b, s, g, d)

    k_cmp = _compress(kv(0), pos_k, ck1, ck2)
    v_cmp = _compress(kv(1), pos_v, cv1, cv2)
    to_t = lambda t: t.transpose(0, 2, 3, 1).astype(BF16)
    to_r = lambda t: t.transpose(0, 2, 1, 3).astype(BF16)
    gl = proj3[:, :, NSA_Q_W + 6 * NSA_KV_W:n_in].reshape(b, s, g, NSA_HPG, 3)
    gl_t = gl.transpose(0, 2, 4, 3, 1).reshape(b, g, 3 * NSA_HPG, s)
    gl_t = jnp.pad(gl_t, ((0, 0), (0, 0), (0, 16 - 3 * NSA_HPG), (0, 0)))
    o = _nsa_attention(proj3, gl_t, k_cmp.transpose(0, 1, 3, 2).astype(BF16), v_cmp.astype(BF16),
                       to_t(kv(2)), to_r(kv(3)), to_t(kv(4)), to_r(kv(5)))
    return _mm(o.reshape(n, -1), w_out, epi='residual', epi_args=(x, gate), rows_per_batch=s,
               name='odd_out_proj')


def _moe(x, mod, norm_g, w_router, b_router, w1, w3, w2, b, s):
    shift, scale, gate = mod
    n = b * s
    h, logits = _norm_router(x, norm_g, scale, shift, w_router, s)
    logits = logits[:, :N_EXPERTS] + b_router.astype(F32)
    top_val, top_idx = lax.top_k(logits, TOP_K)
    gate_w = jax.nn.softmax(top_val, axis=-1)
    flat_e = top_idx.reshape(-1)
    onehot = (flat_e[:, None] == jnp.arange(N_EXPERTS, dtype=jnp.int32)[None, :]).astype(jnp.int32)
    rank = jnp.take_along_axis(jnp.cumsum(onehot, axis=0), flat_e[:, None], axis=1)[:, 0] - 1
    counts = jnp.sum(onehot, axis=0)
    padded = ((counts + MOE_BLOCK - 1) // MOE_BLOCK) * MOE_BLOCK
    pad_end = jnp.cumsum(padded)
    pad_start = pad_end - padded
    dest = pad_start[flat_e] + rank
    n_assign = n * TOP_K
    n_blk = -(-n_assign // MOE_BLOCK) + N_EXPERTS
    n_rows = n_blk * MOE_BLOCK
    flat_tok = jnp.repeat(jnp.arange(n, dtype=jnp.int32), TOP_K)
    row_tok = jnp.zeros((n_rows,), jnp.int32).at[dest].set(flat_tok, unique_indices=True)
    blk_exp = jnp.minimum(jnp.searchsorted(pad_end, jnp.arange(n_blk, dtype=jnp.int32) * MOE_BLOCK,
                                           side='right'), N_EXPERTS - 1).astype(jnp.int32)
    n_used = (pad_end[-1:] // MOE_BLOCK).astype(jnp.int32)
    row_w = jnp.zeros((n_rows,), F32).at[dest].set(gate_w.reshape(-1), unique_indices=True)
    y_rows = _swiglu_grouped(h[row_tok], row_w[:, None], blk_exp, n_used, w1, w3, w2)
    y = jnp.sum(y_rows[dest].reshape(n, TOP_K, -1).astype(F32), axis=1)
    return x + jnp.broadcast_to(gate, (b, s, gate.shape[-1])).reshape(n, -1) * y


def kernel(x, c, positions, ada_w, ada_b, norm_g, final_g, ev_w_in, ev_conv_w, ev_a_log, ev_dt_bias, ev_dn_norm, ev_q_norm, ev_kv_norm, ev_w_uq, ev_w_ukv, ev_w_out, ev_ff_gate, ev_ff_up, ev_ff_down, od_w_in, od_cmp_pos_k, od_cmp_pos_v, od_cmp_k1, od_cmp_k2, od_cmp_v1, od_cmp_v2, od_w_out, od_router, od_router_b, od_moe_w1, od_moe_w3, od_moe_w2):
    b, s, dm = x.shape
    depth = ada_w.shape[0]
    inv = 1.0 / (ROPE_THETA ** (jnp.arange(0, MLA_ROPE, 2, dtype=F32) / MLA_ROPE))
    ang = positions.astype(F32)[..., None] * inv
    cos, sin = jnp.cos(ang), jnp.sin(ang)
    mods = jnp.einsum('bd,lkde->lkbe', jax.nn.silu(c), ada_w,
                      precision=lax.Precision.HIGHEST) + ada_b[:, :, None, :]

    def mod(layer, k):
        m = mods[layer, k]
        return tuple(m[:, None, i * dm:(i + 1) * dm] for i in range(3))

    xf = x.reshape(b * s, dm)
    for layer in range(depth):
        j = layer // 2
        if layer % 2 == 0:
            xf = _even_token_mixer(xf, mod(layer, 0), norm_g[layer, 0], cos, sin, ev_w_in[j],
                                   ev_conv_w[j], ev_a_log[j], ev_dt_bias[j], ev_dn_norm[j],
                                   ev_q_norm[j], ev_kv_norm[j], ev_w_uq[j], ev_w_ukv[j],
                                   ev_w_out[j], b, s)
            shift, scale, gate = mod(layer, 1)
            xf = _swiglu_dense(xf, norm_g[layer, 1], scale, shift, ev_ff_gate[j], ev_ff_up[j],
                               ev_ff_down[j], gate, s)
        else:
            xf = _odd_token_mixer(xf, mod(layer, 0), norm_g[layer, 0], od_w_in[j],
                                  od_cmp_pos_k[j], od_cmp_pos_v[j], od_cmp_k1[j], od_cmp_k2[j],
                                  od_cmp_v1[j], od_cmp_v2[j], od_w_out[j], b, s)
            xf = _moe(xf, mod(layer, 1), norm_g[layer, 1], od_router[j], od_router_b[j],
                      od_moe_w1[j], od_moe_w3[j], od_moe_w2[j], b, s)
    return _final_norm(xf, final_g).reshape(b, s, dm)
```

```python
import functools
import math

import jax
import jax.numpy as jnp
import numpy as np
from jax import lax
from jax.experimental import pallas as pl
from jax.experimental.pallas import tpu as pltpu

F32 = jnp.float32
BF16 = jnp.bfloat16

NORM_EPS = 1e-6
NEG_INF = -1e30
TINY = 1e-30

LANES = 128

DN_HEADS = 4
DN_HEAD_DIM = 128
DN_CONV = 4
DN_CHUNK = 64
DN_W = DN_HEADS * DN_HEAD_DIM

MLA_HEADS = 4
MLA_Q_RANK = 256
MLA_KV_RANK = 128
MLA_NOPE = 128
MLA_ROPE = 64
MLA_V = 128
MLA_QK = MLA_NOPE + MLA_ROPE
ROPE_THETA = 10000.0

NSA_HEADS = 16
NSA_GROUPS = 4
NSA_HPG = NSA_HEADS // NSA_GROUPS
NSA_HEAD_DIM = 64
NSA_Q_W = NSA_HEADS * NSA_HEAD_DIM
NSA_KV_W = NSA_GROUPS * NSA_HEAD_DIM
CMP_LEN = 32
CMP_STRIDE = 16
CMP_HIDDEN = 256
SLC_LEN = 64
SLC_TOPN = 16
WIN = 512
FORCE_SCORE = 1e9

N_EXPERTS = 8
TOP_K = 2
MOE_BLOCK = 512

VMEM_LIMIT = 56 * 1024 * 1024


def _params(sem):
    return pltpu.CompilerParams(dimension_semantics=sem, vmem_limit_bytes=VMEM_LIMIT)


def _sigmoid(x):
    return 1.0 / (1.0 + jnp.exp(-x))


def _silu(x):
    return x * _sigmoid(x)


def _dot(a, b):
    return jnp.dot(a.astype(BF16), b.astype(BF16), preferred_element_type=F32)


def _dot_nt(a, b):
    return lax.dot_general(a.astype(BF16), b.astype(BF16), (((1,), (1,)), ((), ())),
                           preferred_element_type=F32)


def _dot_tn(a, b):
    return lax.dot_general(a.astype(BF16), b.astype(BF16), (((0,), (0,)), ((), ())),
                           preferred_element_type=F32)


def _split3(x):
    hi = x.astype(BF16)
    r1 = x - hi.astype(F32)
    mid = r1.astype(BF16)
    lo = (r1 - mid.astype(F32)).astype(BF16)
    return hi, mid, lo


def _dot_lhs01(a01, x):
    hi, mid, lo = _split3(x)
    a = a01.astype(BF16)
    d = functools.partial(jnp.dot, preferred_element_type=F32)
    return d(a, hi) + d(a, mid) + d(a, lo)


def _dot_rhs01(x, b01):
    hi, mid, lo = _split3(x)
    b = b01.astype(BF16)
    d = functools.partial(jnp.dot, preferred_element_type=F32)
    return d(hi, b) + d(mid, b) + d(lo, b)


def _dot_hi(a, b):
    ah = a.astype(BF16)
    al = (a - ah.astype(F32)).astype(BF16)
    bh = b.astype(BF16)
    bl = (b - bh.astype(F32)).astype(BF16)
    d = functools.partial(jnp.dot, preferred_element_type=F32)
    return d(ah, bh) + d(ah, bl) + d(al, bh)


def _rms(x, gain):
    return x * lax.rsqrt(jnp.mean(x * x, axis=-1, keepdims=True) + NORM_EPS) * gain


def _mm_body(*refs, pro, epi):
    x_ref, w_ref = refs[0], refs[1]
    pos = 2
    if pro == 'adaln':
        g_ref, sc_ref, sh_ref = refs[pos:pos + 3]
        pos += 3
    elif pro == 'rms':
        g_ref = refs[pos]
        pos += 1
    if epi == 'residual':
        res_ref, gate_ref = refs[pos:pos + 2]
        pos += 2
    o_ref = refs[pos]
    pos += 1
    if pro is not None:
        h_ref = refs[pos]

        @pl.when(pl.program_id(1) == 0)
        def _():
            y = _rms(x_ref[...].astype(F32), g_ref[...])
            if pro == 'adaln':
                y = y * (1.0 + sc_ref[...]) + sh_ref[...]
            h_ref[...] = y.astype(BF16)

        h = h_ref[...]
    else:
        h = x_ref[...].astype(BF16)
    acc = jnp.dot(h, w_ref[...], preferred_element_type=F32)
    if epi == 'residual':
        acc = res_ref[...] + gate_ref[...] * acc
    o_ref[...] = acc.astype(o_ref.dtype)


def _pick_tile(n, cap):
    best = LANES
    t = LANES
    while t <= min(n, cap):
        if n % t == 0:
            best = t
        t += LANES
    return best


def _mm(x, w, *, pro=None, pro_args=(), epi=None, epi_args=(), rows_per_batch=None,
        out_dtype=F32, tm=512, tn_cap=1024, name='mm'):
    m, k = x.shape
    n = w.shape[1]
    tm = min(tm, m)
    tn = _pick_tile(n, tn_cap)
    assert m % tm == 0 and n % tn == 0
    rpb = None if rows_per_batch is None else rows_per_batch // tm
    in_specs = [pl.BlockSpec((tm, k), lambda i, j: (i, 0)),
                pl.BlockSpec((k, tn), lambda i, j: (0, j))]
    args = [x, w.astype(BF16)]
    scratch = []
    if pro == 'adaln':
        g, sc, sh = pro_args
        in_specs += [pl.BlockSpec((1, k), lambda i, j: (0, 0)),
                     pl.BlockSpec((None, 1, k), lambda i, j: (i // rpb, 0, 0)),
                     pl.BlockSpec((None, 1, k), lambda i, j: (i // rpb, 0, 0))]
        args += [g.reshape(1, k), sc, sh]
    elif pro == 'rms':
        in_specs += [pl.BlockSpec((1, k), lambda i, j: (0, 0))]
        args += [pro_args[0].reshape(1, k)]
    if pro is not None:
        scratch = [pltpu.VMEM((tm, k), BF16)]
    if epi == 'residual':
        res, gate = epi_args
        in_specs += [pl.BlockSpec((tm, tn), lambda i, j: (i, j)),
                     pl.BlockSpec((None, 1, tn), lambda i, j: (i // rpb, 0, j))]
        args += [res, gate]
    return pl.pallas_call(
        functools.partial(_mm_body, pro=pro, epi=epi),
        out_shape=jax.ShapeDtypeStruct((m, n), out_dtype),
        grid=(m // tm, n // tn),
        in_specs=in_specs,
        out_specs=pl.BlockSpec((tm, tn), lambda i, j: (i, j)),
        scratch_shapes=scratch,
        compiler_params=_params(("parallel", "arbitrary")),
        name=name,
    )(*args)


def _swiglu_dense_body(x_ref, g_ref, sc_ref, sh_ref, wg_ref, wu_ref, wd_ref, gate_ref, o_ref,
                       h_ref, acc_ref):
    j = pl.program_id(1)

    @pl.when(j == 0)
    def _():
        y = _rms(x_ref[...], g_ref[...]) * (1.0 + sc_ref[...]) + sh_ref[...]
        h_ref[...] = y.astype(BF16)
        acc_ref[...] = jnp.zeros_like(acc_ref)

    h = h_ref[...]
    a = jnp.dot(h, wg_ref[...], preferred_element_type=F32)
    b = jnp.dot(h, wu_ref[...], preferred_element_type=F32)
    act = (_silu(a) * b).astype(BF16)
    acc_ref[...] += jnp.dot(act, wd_ref[...], preferred_element_type=F32)

    @pl.when(j == pl.num_programs(1) - 1)
    def _():
        o_ref[...] = x_ref[...] + gate_ref[...] * acc_ref[...]


def _swiglu_dense(x, g, sc, sh, wg, wu, wd, gate, rows_per_batch, *, tm=1024, tf=256):
    m, k = x.shape
    f = wg.shape[1]
    assert m % tm == 0 and f % tf == 0
    rpb = rows_per_batch // tm
    return pl.pallas_call(
        _swiglu_dense_body,
        out_shape=jax.ShapeDtypeStruct((m, k), F32),
        grid=(m // tm, f // tf),
        in_specs=[pl.BlockSpec((tm, k), lambda i, j: (i, 0)),
                  pl.BlockSpec((1, k), lambda i, j: (0, 0)),
                  pl.BlockSpec((None, 1, k), lambda i, j: (i // rpb, 0, 0)),
                  pl.BlockSpec((None, 1, k), lambda i, j: (i // rpb, 0, 0)),
                  pl.BlockSpec((k, tf), lambda i, j: (0, j)),
                  pl.BlockSpec((k, tf), lambda i, j: (0, j)),
                  pl.BlockSpec((tf, k), lambda i, j: (j, 0)),
                  pl.BlockSpec((None, 1, k), lambda i, j: (i // rpb, 0, 0))],
        out_specs=pl.BlockSpec((tm, k), lambda i, j: (i, 0)),
        scratch_shapes=[pltpu.VMEM((tm, k), BF16), pltpu.VMEM((tm, k), F32)],
        compiler_params=_params(("parallel", "arbitrary")),
        name='swiglu_dense',
    )(x, g.reshape(1, k), sc, sh, wg.astype(BF16), wu.astype(BF16), wd.astype(BF16), gate)


def _swiglu_grouped_body(be_ref, nu_ref, x_ref, rw_ref, wg_ref, wu_ref, wd_ref, o_ref, acc_ref):
    i = pl.program_id(0)
    j = pl.program_id(1)
    used = i < nu_ref[0]

    @pl.when(j == 0)
    def _():
        acc_ref[...] = jnp.zeros_like(acc_ref)

    @pl.when(used)
    def _():
        h = x_ref[...]
        a = jnp.dot(h, wg_ref[...], preferred_element_type=F32)
        b = jnp.dot(h, wu_ref[...], preferred_element_type=F32)
        act = (_silu(a) * b).astype(BF16)
        acc_ref[...] += jnp.dot(act, wd_ref[...], preferred_element_type=F32)

    @pl.when(j == pl.num_programs(1) - 1)
    def _():
        o_ref[...] = (acc_ref[...] * rw_ref[...]).astype(o_ref.dtype)


def _swiglu_grouped(x_rows, row_w, blk_exp, n_used, w1, w3, w2, *, tm=MOE_BLOCK, tf=512):
    n_rows, k = x_rows.shape
    f = w1.shape[2]
    assert n_rows % tm == 0 and f % tf == 0
    nj = f // tf

    def jj(i, j, nu):
        return jnp.where(i < nu[0], j, nj - 1)

    grid_spec = pltpu.PrefetchScalarGridSpec(
        num_scalar_prefetch=2,
        grid=(n_rows // tm, nj),
        in_specs=[pl.BlockSpec((tm, k), lambda i, j, be, nu: (i, 0)),
                  pl.BlockSpec((tm, 1), lambda i, j, be, nu: (i, 0)),
                  pl.BlockSpec((None, k, tf), lambda i, j, be, nu: (be[i], 0, jj(i, j, nu))),
                  pl.BlockSpec((None, k, tf), lambda i, j, be, nu: (be[i], 0, jj(i, j, nu))),
                  pl.BlockSpec((None, tf, k), lambda i, j, be, nu: (be[i], jj(i, j, nu), 0))],
        out_specs=pl.BlockSpec((tm, k), lambda i, j, be, nu: (i, 0)),
        scratch_shapes=[pltpu.VMEM((tm, k), F32)],
    )
    return pl.pallas_call(
        _swiglu_grouped_body,
        out_shape=jax.ShapeDtypeStruct((n_rows, k), BF16),
        grid_spec=grid_spec,
        compiler_params=_params(("arbitrary", "arbitrary")),
        name='swiglu_grouped',
    )(blk_exp, n_used, x_rows, row_w, w1.astype(BF16), w3.astype(BF16), w2.astype(BF16))


def _moe_combine_body(x_ref, y_ref, gate_ref, o_ref):
    k = x_ref.shape[-1]
    y = y_ref[...]
    total = y[:, :k].astype(F32)
    for j in range(1, y.shape[-1] // k):
        total = total + y[:, j * k:(j + 1) * k].astype(F32)
    o_ref[...] = x_ref[...] + gate_ref[...] * total


def _moe_combine(x, y_tok, gate, rows_per_batch, *, tm=512):
    m, k = x.shape
    rpb = rows_per_batch // tm
    return pl.pallas_call(
        _moe_combine_body,
        out_shape=jax.ShapeDtypeStruct((m, k), F32),
        grid=(m // tm,),
        in_specs=[pl.BlockSpec((tm, k), lambda i: (i, 0)),
                  pl.BlockSpec((tm, y_tok.shape[1]), lambda i: (i, 0)),
                  pl.BlockSpec((None, 1, k), lambda i: (i // rpb, 0, 0))],
        out_specs=pl.BlockSpec((tm, k), lambda i: (i, 0)),
        compiler_params=_params(("parallel",)),
        name='moe_combine',
    )(x, y_tok, gate)


def _norm_router_body(x_ref, g_ref, sc_ref, sh_ref, wr_ref, h_ref, lg_ref):
    y = _rms(x_ref[...], g_ref[...]) * (1.0 + sc_ref[...]) + sh_ref[...]
    h_ref[...] = y.astype(BF16)
    lg_ref[...] = jnp.dot(y, wr_ref[...], preferred_element_type=F32,
                          precision=lax.Precision.HIGHEST)


def _norm_router(x, g, sc, sh, w_router, rows_per_batch, *, tm=512):
    m, k = x.shape
    e = w_router.shape[1]
    wr = jnp.zeros((k, LANES), F32).at[:, :e].set(w_router)
    rpb = rows_per_batch // tm
    return pl.pallas_call(
        _norm_router_body,
        out_shape=(jax.ShapeDtypeStruct((m, k), BF16), jax.ShapeDtypeStruct((m, LANES), F32)),
        grid=(m // tm,),
        in_specs=[pl.BlockSpec((tm, k), lambda i: (i, 0)),
                  pl.BlockSpec((1, k), lambda i: (0, 0)),
                  pl.BlockSpec((None, 1, k), lambda i: (i // rpb, 0, 0)),
                  pl.BlockSpec((None, 1, k), lambda i: (i // rpb, 0, 0)),
                  pl.BlockSpec((k, LANES), lambda i: (0, 0))],
        out_specs=(pl.BlockSpec((tm, k), lambda i: (i, 0)),
                   pl.BlockSpec((tm, LANES), lambda i: (i, 0))),
        compiler_params=_params(("parallel",)),
        name='norm_router',
    )(x, g.reshape(1, k), sc, sh, wr)


def _final_norm_body(x_ref, g_ref, o_ref):
    o_ref[...] = _rms(x_ref[...], g_ref[...])


def _final_norm(x, g, *, tm=1024):
    m, k = x.shape
    return pl.pallas_call(
        _final_norm_body,
        out_shape=jax.ShapeDtypeStruct((m, k), F32),
        grid=(m // tm,),
        in_specs=[pl.BlockSpec((tm, k), lambda i: (i, 0)), pl.BlockSpec((1, k), lambda i: (0, 0))],
        out_specs=pl.BlockSpec((tm, k), lambda i: (i, 0)),
        compiler_params=_params(("parallel",)),
        name='final_norm',
    )(x, g.reshape(1, k))


def _softplus(x):
    return jnp.maximum(x, 0.0) + jnp.log(1.0 + jnp.exp(-jnp.abs(x)))


CONV_HALO = 8


def _deltanet_body(q_ref, k_ref, v_ref, z_ref, pb_ref, pa_ref, cwq_ref, cwk_ref, cwv_ref, alog_ref,
                   dtb_ref, gn_ref, o_ref, s_ref, xs_ref, *, n_chunks):
    c_len = DN_CHUNK
    tile = n_chunks * c_len

    @pl.when(pl.program_id(1) == 0)
    def _():
        s_ref[...] = jnp.zeros_like(s_ref)
        xs_ref[:, 0:CONV_HALO, :] = jnp.zeros((3, CONV_HALO, xs_ref.shape[-1]), F32)

    def conv(i, x_ref, cw_ref):
        xs_ref[i, CONV_HALO:CONV_HALO + tile, :] = x_ref[...]
        y = None
        for tap in range(DN_CONV):
            off = CONV_HALO - (DN_CONV - 1) + tap
            term = xs_ref[i, off:off + tile, :] * cw_ref[tap:tap + 1, :]
            y = term if y is None else y + term
        xs_ref[i, 0:CONV_HALO, :] = xs_ref[i, tile:tile + CONV_HALO, :]
        return _silu(y)

    q = conv(0, q_ref, cwq_ref)
    k = conv(1, k_ref, cwk_ref)
    v = conv(2, v_ref, cwv_ref)

    row = lax.broadcasted_iota(jnp.int32, (c_len, c_len), 0)
    col = lax.broadcasted_iota(jnp.int32, (c_len, c_len), 1)
    incl = row >= col
    strict = row > col
    incl_f = incl.astype(F32)
    strict_f = strict.astype(F32)
    eye = (row == col).astype(F32)
    q = q * lax.rsqrt(jnp.sum(q * q, axis=-1, keepdims=True) + NORM_EPS) * (DN_HEAD_DIM ** -0.5)
    k = k * lax.rsqrt(jnp.sum(k * k, axis=-1, keepdims=True) + NORM_EPS)
    beta = _sigmoid(pb_ref[...])
    g = -jnp.exp(alog_ref[...]) * _softplus(pa_ref[...] + dtb_ref[...])
    kb = k * beta
    vb = v * beta
    chunks = range(n_chunks)
    cs = lambda x, c: x[c * c_len:(c + 1) * c_len]
    gc = [_dot_lhs01(incl_f, cs(g, c)) for c in chunks]
    gdiff = [_dot_lhs01(incl_f, cs(g, c)[:, :c_len] * strict_f) for c in chunks]
    decay = [jnp.where(incl, jnp.exp(jnp.where(incl, gd, 0.0)), 0.0) for gd in gdiff]
    eg = [jnp.exp(x) for x in gc]
    lower = [jnp.where(strict, _dot_nt(cs(kb, c), cs(k, c)) * decay[c], 0.0) for c in chunks]
    tinv = [eye - lo for lo in lower]
    pw = lower
    for it in range(5):
        mul = _dot_hi if it == 0 else _dot
        pw = [mul(p, p) for p in pw]
        tinv = [t + mul(t, p) for t, p in zip(tinv, pw)]
    u = [_dot(tinv[c], cs(vb, c)) for c in chunks]
    w = [_dot(tinv[c], cs(kb, c) * eg[c]) for c in chunks]
    qk = [_dot_nt(cs(q, c), cs(k, c)) * decay[c] for c in chunks]
    g_last = [x[c_len - 1:c_len, :] for x in gc]
    k_tail = [cs(k, c) * jnp.exp(g_last[c] - gc[c]) for c in chunks]
    m_mat = [_dot_tn(k_tail[c], w[c]) for c in chunks]
    b_mat = [_dot_tn(k_tail[c], u[c]) for c in chunks]
    q_eff = [cs(q, c) * eg[c] - _dot(qk[c], w[c]) for c in chunks]
    o_loc = [_dot(qk[c], u[c]) for c in chunks]
    state = s_ref[...]
    outs = []
    for c in chunks:
        outs.append(_dot(q_eff[c], state) + o_loc[c])
        state = state * jnp.exp(g_last[c]) - _dot(m_mat[c], state) + b_mat[c]
    s_ref[...] = state
    o = jnp.concatenate(outs, axis=0)
    z = z_ref[...]
    o_ref[...] = (_rms(o, gn_ref[...]) * _silu(z)).astype(o_ref.dtype)


def _deltanet(proj, conv_w, a_log, dt_bias, dn_norm, *, z_blk, pb_blk, pa_blk, tile=512):
    b, s, _ = proj.shape
    h = DN_HEADS
    d = DN_HEAD_DIM
    tile = min(tile, s)
    assert s % tile == 0 and tile % DN_CHUNK == 0
    rep = lambda t: jnp.broadcast_to(t.astype(F32)[:, None, None], (h, 1, d))

    def col(off):
        return pl.BlockSpec((None, tile, d), lambda bh, t, off=off: (bh // h, t, off + bh % h))

    def cw(off):
        return pl.BlockSpec((DN_CONV, d), lambda bh, t, off=off: (0, off + bh % h))

    per_head = pl.BlockSpec((None, 1, d), lambda bh, t: (bh % h, 0, 0))
    return pl.pallas_call(
        functools.partial(_deltanet_body, n_chunks=tile // DN_CHUNK),
        out_shape=jax.ShapeDtypeStruct((b, s, h * d), BF16),
        grid=(b * h, s // tile),
        in_specs=[col(0), col(h), col(2 * h), col(z_blk), col(pb_blk), col(pa_blk),
                  cw(0), cw(h), cw(2 * h),
                  per_head, per_head, pl.BlockSpec((1, d), lambda bh, t: (0, 0))],
        out_specs=col(0),
        scratch_shapes=[pltpu.VMEM((d, d), F32), pltpu.VMEM((3, tile + CONV_HALO, d), F32)],
        compiler_params=_params(("parallel", "arbitrary")),
        name='deltanet',
    )(proj, proj, proj, proj, proj, proj, conv_w, conv_w, conv_w, rep(a_log), rep(dt_bias),
      dn_norm.reshape(1, d))


LOG2E = 1.4426950408889634


def _rope_halves(x1, x2, cos, sin):
    return x1 * cos - x2 * sin, x2 * cos + x1 * sin


def _mla_q_up_body(cq_ref, g_ref, w_ref, cos_ref, sin_ref, o_ref):
    hm, half = MLA_HEADS, MLA_ROPE // 2
    x = _rms(cq_ref[...], g_ref[...]).astype(BF16)
    acc = jnp.dot(x, w_ref[...], preferred_element_type=F32)
    nope_w = hm * MLA_NOPE
    r1, r2 = _rope_halves(acc[:, nope_w:nope_w + hm * half], acc[:, nope_w + hm * half:],
                          cos_ref[...], sin_ref[...])
    for h in range(hm):
        q_h = jnp.concatenate([acc[:, h * MLA_NOPE:(h + 1) * MLA_NOPE],
                               r1[:, h * half:(h + 1) * half], r2[:, h * half:(h + 1) * half]], axis=1)
        o_ref[h] = (q_h * (MLA_QK ** -0.5 * LOG2E)).astype(o_ref.dtype)


def _mla_q_up(proj, q_norm, w_uq, cos_t, sin_t, b, s, *, cq_blk, tm=512):
    n = proj.shape[0]
    hm, half = MLA_HEADS, MLA_ROPE // 2
    assert hm * half == LANES and s % tm == 0
    heads = np.arange(hm)[:, None] * MLA_QK
    cols = np.concatenate([(heads + np.arange(MLA_NOPE)[None, :]).ravel(),
                           (heads + MLA_NOPE + np.arange(half)[None, :]).ravel(),
                           (heads + MLA_NOPE + half + np.arange(half)[None, :]).ravel()])
    rpb = s // tm
    return pl.pallas_call(
        _mla_q_up_body,
        out_shape=jax.ShapeDtypeStruct((b, hm, s, MLA_QK), BF16),
        grid=(n // tm,),
        in_specs=[pl.BlockSpec((tm, MLA_Q_RANK), lambda i: (i, cq_blk)),
                  pl.BlockSpec((1, MLA_Q_RANK), lambda i: (0, 0)),
                  pl.BlockSpec((MLA_Q_RANK, hm * MLA_QK), lambda i: (0, 0)),
                  pl.BlockSpec((tm, LANES), lambda i: (i, 0)),
                  pl.BlockSpec((tm, LANES), lambda i: (i, 0))],
        out_specs=pl.BlockSpec((None, hm, tm, MLA_QK), lambda i: (i // rpb, 0, i % rpb, 0)),
        compiler_params=_params(("parallel",)),
        name='mla_q_up',
    )(proj, q_norm.reshape(1, -1), w_uq[:, cols].astype(BF16), cos_t, sin_t)


def _mla_kv_up_body(ckv_ref, kr_ref, g_ref, w_ref, cos_ref, sin_ref, k_ref, v_ref):
    hm, half = MLA_HEADS, MLA_ROPE // 2
    x = _rms(ckv_ref[...], g_ref[...]).astype(BF16)
    acc = jnp.dot(x, w_ref[...], preferred_element_type=F32)
    kr = kr_ref[...]
    r1, r2 = _rope_halves(kr[:, :half], kr[:, half:2 * half], cos_ref[:, :half], sin_ref[:, :half])
    for h in range(hm):
        k_h = jnp.concatenate([acc[:, h * MLA_NOPE:(h + 1) * MLA_NOPE], r1, r2], axis=1)
        k_ref[h] = k_h.astype(k_ref.dtype)
        v_lo = hm * MLA_NOPE + h * MLA_V
        v_ref[h] = acc[:, v_lo:v_lo + MLA_V].astype(v_ref.dtype)


def _mla_kv_up(proj, kv_norm, w_ukv, cos_t, sin_t, b, s, *, ckv_blk, kr_blk, tm=512):
    n = proj.shape[0]
    hm = MLA_HEADS
    assert MLA_KV_RANK == LANES and MLA_V == LANES and s % tm == 0
    heads = np.arange(hm)[:, None] * (MLA_NOPE + MLA_V)
    cols = np.concatenate([(heads + np.arange(MLA_NOPE)[None, :]).ravel(),
                           (heads + MLA_NOPE + np.arange(MLA_V)[None, :]).ravel()])
    rpb = s // tm
    head_major = lambda w: pl.BlockSpec((None, hm, tm, w), lambda i: (i // rpb, 0, i % rpb, 0))
    return pl.pallas_call(
        _mla_kv_up_body,
        out_shape=(jax.ShapeDtypeStruct((b, hm, s, MLA_QK), BF16),
                   jax.ShapeDtypeStruct((b, hm, s, MLA_V), BF16)),
        grid=(n // tm,),
        in_specs=[pl.BlockSpec((tm, LANES), lambda i: (i, ckv_blk)),
                  pl.BlockSpec((tm, LANES), lambda i: (i, kr_blk)),
                  pl.BlockSpec((1, LANES), lambda i: (0, 0)),
                  pl.BlockSpec((MLA_KV_RANK, hm * (MLA_NOPE + MLA_V)), lambda i: (0, 0)),
                  pl.BlockSpec((tm, LANES), lambda i: (i, 0)),
                  pl.BlockSpec((tm, LANES), lambda i: (i, 0))],
        out_specs=(head_major(MLA_QK), head_major(MLA_V)),
        compiler_params=_params(("parallel",)),
        name='mla_kv_up',
    )(proj, proj, kv_norm.reshape(1, -1), w_ukv[:, cols].astype(BF16), cos_t, sin_t)


def _mla_body(q_ref, k_ref, v_ref, o_ref, *, tq, tk):
    qi = pl.program_id(2)
    q = q_ref[...]
    dv = o_ref.shape[-1]

    def step(kt, carry, masked):
        m, l, acc = carry
        ks = pl.multiple_of(kt * tk, tk)
        k = k_ref[pl.ds(ks, tk), :]
        v = v_ref[pl.ds(ks, tk), :]
        s = lax.dot_general(q, k, (((1,), (1,)), ((), ())), preferred_element_type=F32)
        if masked:
            tpos = qi * tq + lax.broadcasted_iota(jnp.int32, (tq, tk), 0)
            kpos = ks + lax.broadcasted_iota(jnp.int32, (tq, tk), 1)
            s = jnp.where(kpos <= tpos, s, NEG_INF)
        m_new = jnp.maximum(m, jnp.max(s, axis=-1, keepdims=True))
        alpha = jnp.exp2(m - m_new)
        p = jnp.exp2(s - m_new)
        l = alpha * l + jnp.sum(p, axis=-1, keepdims=True)
        acc = alpha * acc + jnp.dot(p.astype(BF16), v, preferred_element_type=F32)
        return m_new, l, acc

    init = (jnp.full((tq, 1), NEG_INF, F32), jnp.zeros((tq, 1), F32), jnp.zeros((tq, dv), F32))
    n_full = qi * (tq // tk)
    carry = lax.fori_loop(0, n_full, lambda kt, c: step(kt, c, False), init)
    for d in range(tq // tk):
        carry = step(n_full + d, carry, True)
    _, l, acc = carry
    o_ref[...] = (acc / l).astype(o_ref.dtype)


def _mla_attention(q, k, v, *, tq=512, tk=512):
    b, h, s, dqk = q.shape
    dva = v.shape[-1]
    dv = dva
    tq = min(tq, s)
    tk = min(tk, tq)
    assert s % tq == 0 and tq % tk == 0
    return pl.pallas_call(
        functools.partial(_mla_body, tq=tq, tk=tk),
        out_shape=jax.ShapeDtypeStruct((b, s, h * dv), BF16),
        grid=(b, h, s // tq),
        in_specs=[pl.BlockSpec((None, None, tq, dqk), lambda bi, hi, qi: (bi, hi, qi, 0)),
                  pl.BlockSpec((None, None, s, dqk), lambda bi, hi, qi: (bi, hi, 0, 0)),
                  pl.BlockSpec((None, None, s, dva), lambda bi, hi, qi: (bi, hi, 0, 0))],
        out_specs=pl.BlockSpec((None, tq, dv), lambda bi, hi, qi: (bi, qi, hi)),
        compiler_params=_params(("parallel", "parallel", "arbitrary")),
        name='mla_attention',
    )(q, k, v)


SUBLANES = 8


def _compress_body(x_ref, pos_ref, w1_ref, w2_ref, o_ref, sh_ref):
    n, half = x_ref.shape
    x = x_ref[...].astype(F32)
    first = jnp.dot((x + pos_ref[0:1, :]).astype(BF16), w1_ref[0:half, :],
                    preferred_element_type=F32)
    second = jnp.dot((x + pos_ref[1:2, :]).astype(BF16), w1_ref[half:2 * half, :],
                     preferred_element_type=F32)
    sh_ref[0:n, :] = second
    sh_ref[n:n + SUBLANES, :] = jnp.zeros((SUBLANES, sh_ref.shape[-1]), F32)
    hid = _silu(first + sh_ref[1:n + 1, :])
    o_ref[...] = jnp.dot(hid.astype(BF16), w2_ref[...], preferred_element_type=F32)


def _compress(t, pos_emb, w1, w2):
    assert CMP_LEN == 2 * CMP_STRIDE
    b, s, g, d = t.shape
    n_chunk = s // CMP_STRIDE
    half = CMP_STRIDE * d
    ch = t.reshape(b, n_chunk, CMP_STRIDE, g, d).transpose(0, 3, 1, 2, 4).reshape(b, g, n_chunk, half)
    w2p = jnp.zeros((CMP_HIDDEN, LANES), BF16).at[:, :d].set(w2.astype(BF16))
    out = pl.pallas_call(
        _compress_body,
        out_shape=jax.ShapeDtypeStruct((b, g, n_chunk, LANES), F32),
        grid=(b, g),
        in_specs=[pl.BlockSpec((None, None, n_chunk, half), lambda bi, gi: (bi, gi, 0, 0)),
                  pl.BlockSpec((2, half), lambda bi, gi: (0, 0)),
                  pl.BlockSpec((2 * half, CMP_HIDDEN), lambda bi, gi: (0, 0)),
                  pl.BlockSpec((CMP_HIDDEN, LANES), lambda bi, gi: (0, 0))],
        out_specs=pl.BlockSpec((None, None, n_chunk, LANES), lambda bi, gi: (bi, gi, 0, 0)),
        scratch_shapes=[pltpu.VMEM((n_chunk + SUBLANES, CMP_HIDDEN), F32)],
        compiler_params=_params(("parallel", "parallel")),
        name='nsa_compress',
    )(ch, pos_emb.reshape(2, half), w1.astype(BF16), w2p)
    return out[..., :d]


MASK_BIG = 1e30


def _nsa_body(q_ref, gl_ref, kct_ref, vc_ref, kse_ref, vs_ref, kwt_ref, vw_ref, o_ref, *,
              tq, tk, n_blocks, n_top, span):
    hg = NSA_HPG
    d = NSA_HEAD_DIM
    rows = hg * tq
    qi = pl.program_id(2)
    q0 = qi * tq
    qf = q_ref[...].astype(F32) * (d ** -0.5 * LOG2E)
    q = jnp.concatenate([qf[:, h * d:(h + 1) * d] for h in range(hg)], axis=0).astype(BF16)
    t_q = q0 + lax.broadcasted_iota(jnp.int32, (tq, 1), 0)

    def add_bias(s, bias):
        n = s.shape[-1]
        return (s.reshape(hg, tq, n) + bias[None]).reshape(rows, n)


    n_cmp = kct_ref.shape[-1]
    cmp_end = lax.broadcasted_iota(jnp.int32, (1, n_cmp), 1) * CMP_STRIDE + (CMP_LEN - 1)
    bias_c = jnp.where(cmp_end <= t_q, 0.0, NEG_INF)
    s_c = add_bias(jnp.dot(q, kct_ref[...], preferred_element_type=F32), bias_c)
    e_c = jnp.exp2(s_c - jnp.max(s_c, axis=-1, keepdims=True))
    acc_c = jnp.dot(e_c.astype(BF16), vc_ref[...], preferred_element_type=F32)
    has_c = jnp.concatenate([t_q >= CMP_LEN - 1] * hg, axis=0)
    inv_c = jnp.where(has_c, 1.0 / jnp.maximum(acc_c[:, d:d + 1], TINY), 0.0)
    o_c = acc_c[:, :d] * inv_c

    p_sum = jnp.sum((e_c * inv_c).reshape(hg, tq, n_cmp), axis=0)
    jj = lax.broadcasted_iota(jnp.int32, (n_cmp, LANES), 0)
    nn = lax.broadcasted_iota(jnp.int32, (n_cmp, LANES), 1)
    per = SLC_LEN // CMP_STRIDE
    pool = 0.5 * ((jj // per == nn).astype(F32) + ((jj + 1) // per == nn).astype(F32))
    imp = _dot_rhs01(p_sum, pool)
    sel_parts = []
    blk = lax.broadcasted_iota(jnp.int32, (LANES, LANES), 0)
    blk_f = blk.astype(F32)
    for r in range(tq // LANES):
        imp_t = imp[r * LANES:(r + 1) * LANES].T
        cur = (q0 + r * LANES + lax.broadcasted_iota(jnp.int32, (1, LANES), 1)) // SLC_LEN
        forced = (blk == 0) | (blk == cur) | (blk == cur - 1)
        imp_t = jnp.where(forced, FORCE_SCORE, imp_t)
        imp_t = jnp.where(blk <= cur, imp_t, -1.0)
        imp_t = jnp.where(blk < n_blocks, imp_t, -3.0)
        sel_t = jnp.zeros((LANES, LANES), F32)
        for _ in range(n_top):
            top = jnp.max(imp_t, axis=0, keepdims=True)
            first = jnp.min(jnp.where(imp_t == top, blk_f, float(LANES)), axis=0, keepdims=True)
            hit = blk_f == first
            sel_t = jnp.where(hit, 1.0, sel_t)
            imp_t = jnp.where(hit, -2.0, imp_t)
        sel_parts.append(sel_t.T)
    sel = jnp.concatenate(sel_parts, axis=0)
    unsel = jnp.concatenate([sel - 1.0] * hg, axis=0)
    q_aug = jnp.concatenate([unsel.astype(BF16), q], axis=1)
    t_rel = t_q - lax.broadcasted_iota(jnp.int32, (1, tk), 1)

    def sel_tile(kt, carry, causal):
        m, acc = carry
        ks = pl.multiple_of(kt * tk, tk)
        s = jnp.dot(q_aug, kse_ref[:, pl.ds(ks, tk)], preferred_element_type=F32)
        if causal:
            s = add_bias(s, jnp.where(t_rel >= ks, 0.0, NEG_INF))
        m_new = jnp.maximum(m, jnp.max(s, axis=-1, keepdims=True))
        p = jnp.exp2(s - m_new)
        acc = jnp.exp2(m - m_new) * acc + jnp.dot(p.astype(BF16), vs_ref[pl.ds(ks, tk), :],
                                                  preferred_element_type=F32)
        return m_new, acc

    init = (jnp.full((rows, 1), NEG_INF, F32), jnp.zeros((rows, vs_ref.shape[-1]), F32))
    n_kt = (q0 + tq + tk - 1) // tk
    carry = lax.fori_loop(0, n_kt - 1, lambda kt, c: sel_tile(kt, c, False), init)
    _, acc_s = sel_tile(n_kt - 1, carry, True)
    o_s = acc_s[:, :d] / jnp.maximum(acc_s[:, d:d + 1], TINY)

    ws = pl.multiple_of(jnp.maximum(q0 + tq - span, 0), LANES)
    dist = (t_q - ws) - lax.broadcasted_iota(jnp.int32, (1, span), 1)
    bias_w = jnp.where((dist >= 0) & (dist < WIN), 0.0, NEG_INF)
    s_w = add_bias(jnp.dot(q, kwt_ref[:, pl.ds(ws, span)], preferred_element_type=F32), bias_w)
    e_w = jnp.exp2(s_w - jnp.max(s_w, axis=-1, keepdims=True))
    acc_w = jnp.dot(e_w.astype(BF16), vw_ref[pl.ds(ws, span), :], preferred_element_type=F32)
    o_w = acc_w[:, :d] / acc_w[:, d:d + 1]

    gates = _sigmoid(gl_ref[...])
    g_t = []
    for r in range(tq // LANES):
        blk_g = jnp.concatenate([gates[:, r * LANES:(r + 1) * LANES],
                                 jnp.zeros((LANES - gates.shape[0], LANES), F32)], axis=0)
        g_t.append(blk_g.T)
    g_t = jnp.concatenate(g_t, axis=0)

    def gate_col(branch):
        return jnp.concatenate([g_t[:, branch * hg + h:branch * hg + h + 1] for h in range(hg)], axis=0)

    o = gate_col(0) * o_c + gate_col(1) * o_s + gate_col(2) * o_w
    o_ref[...] = jnp.concatenate([o[h * tq:(h + 1) * tq] for h in range(hg)], axis=1).astype(o_ref.dtype)


def _nsa_attention(proj, gl_t, kct, vc, kst, vs, kwt, vw, *, tq=256, tk=512):
    b, s, _ = proj.shape
    g = NSA_GROUPS
    d = NSA_HEAD_DIM
    n_cmp = kct.shape[-1]
    tq = min(tq, s)
    tk = min(tk, s)
    span = min(WIN + tq, s)
    n_blocks = s // SLC_LEN
    assert s % tq == 0 and s % tk == 0 and tq % LANES == 0 and n_blocks <= LANES
    kv_t = lambda n, r=d: pl.BlockSpec((None, None, r, n), lambda bi, gi, qi: (bi, gi, 0, 0))
    kv_r = lambda n: pl.BlockSpec((None, None, n, LANES), lambda bi, gi, qi: (bi, gi, 0, 0))

    def with_ones(v):
        tail = jnp.zeros(v.shape[:-1] + (LANES - d,), BF16).at[..., 0].set(1.0)
        return jnp.concatenate([v, tail], axis=-1)

    blk_of_key = jnp.arange(s, dtype=jnp.int32) // SLC_LEN
    mask_rows = jnp.where(jnp.arange(LANES, dtype=jnp.int32)[:, None] == blk_of_key[None, :],
                          MASK_BIG, 0.0).astype(BF16)
    kse = jnp.concatenate([jnp.broadcast_to(mask_rows, (b, g, LANES, s)), kst], axis=2)
    vc, vs, vw = with_ones(vc), with_ones(vs), with_ones(vw)
    kst = kse
    return pl.pallas_call(
        functools.partial(_nsa_body, tq=tq, tk=tk, n_blocks=n_blocks,
                          n_top=min(SLC_TOPN, n_blocks), span=span),
        out_shape=jax.ShapeDtypeStruct((b, s, g * NSA_HPG * d), BF16),
        grid=(b, g, s // tq),
        in_specs=[pl.BlockSpec((None, tq, NSA_HPG * d), lambda bi, gi, qi: (bi, qi, gi)),
                  pl.BlockSpec((None, None, 16, tq), lambda bi, gi, qi: (bi, gi, 0, qi)),
                  kv_t(n_cmp), kv_r(n_cmp), kv_t(s, LANES + d), kv_r(s), kv_t(s), kv_r(s)],
        out_specs=pl.BlockSpec((None, tq, NSA_HPG * d), lambda bi, gi, qi: (bi, qi, gi)),
        compiler_params=_params(("parallel", "parallel", "arbitrary")),
        name='nsa_attention',
    )(proj, gl_t, kct, vc, kst, vs, kwt, vw)


def _pad_cols(w, n):
    return jnp.pad(w, ((0, 0), (0, n - w.shape[1])))


def _even_token_mixer(x, mod, norm_g, cos, sin, w_in, conv_w, a_log, dt_bias, dn_norm, q_norm,
                      kv_norm, w_uq, w_ukv, w_out, b, s):
    shift, scale, gate = mod
    n = b * s
    hd = DN_HEADS
    o_b = 4 * DN_W
    o_a = o_b + hd
    o_cq = o_a + hd
    o_ckv = o_cq + MLA_Q_RANK
    o_kr = o_ckv + MLA_KV_RANK
    w_cols = [w_in[:, :o_b], w_in[:, o_cq:o_ckv], w_in[:, o_ckv:o_kr],
              _pad_cols(w_in[:, o_kr:o_kr + MLA_ROPE], LANES),
              jnp.repeat(w_in[:, o_b:o_a], DN_HEAD_DIM, axis=1),
              jnp.repeat(w_in[:, o_a:o_cq], DN_HEAD_DIM, axis=1)]
    w_all = jnp.concatenate(w_cols, axis=1)
    proj = _mm(x, w_all, pro='adaln', pro_args=(norm_g, scale, shift), rows_per_batch=s,
               tn_cap=1024, name='even_in_proj')
    width = proj.shape[1]
    c0 = 4 * DN_W
    pb_blk = (c0 + MLA_Q_RANK + MLA_KV_RANK + LANES) // LANES
    proj3 = proj.reshape(b, s, width)

    o_dn = _deltanet(proj3, conv_w.astype(F32), a_log, dt_bias, dn_norm, z_blk=3 * hd,
                     pb_blk=pb_blk, pa_blk=pb_blk + hd)

    reps = LANES // (MLA_ROPE // 2)
    cos_t = jnp.tile(cos.reshape(n, -1), (1, reps))
    sin_t = jnp.tile(sin.reshape(n, -1), (1, reps))
    q_h = _mla_q_up(proj, q_norm, w_uq, cos_t, sin_t, b, s, cq_blk=c0 // MLA_Q_RANK)
    k_h, v_h = _mla_kv_up(proj, kv_norm, w_ukv, cos_t, sin_t, b, s,
                          ckv_blk=(c0 + MLA_Q_RANK) // LANES,
                          kr_blk=(c0 + MLA_Q_RANK + MLA_KV_RANK) // LANES)
    o_mla = _mla_attention(q_h, k_h, v_h)
    mix = jnp.concatenate([o_dn, o_mla], axis=-1).reshape(n, -1)
    return _mm(mix, w_out, epi='residual', epi_args=(x, gate), rows_per_batch=s, name='even_out_proj')


def _odd_token_mixer(x, mod, norm_g, w_in, pos_k, pos_v, ck1, ck2, cv1, cv2, w_out, b, s):
    shift, scale, gate = mod
    n = b * s
    g = NSA_GROUPS
    d = NSA_HEAD_DIM
    n_in = NSA_Q_W + 6 * NSA_KV_W + 3 * NSA_HEADS
    w_all = _pad_cols(w_in, -(-n_in // LANES) * LANES)
    proj = _mm(x, w_all, pro='adaln', pro_args=(norm_g, scale, shift), rows_per_batch=s,
               tn_cap=1024, out_dtype=BF16, name='odd_in_proj')
    proj3 = proj.reshape(b, s, -1)

    def kv(i):
        lo = NSA_Q_W + i * NSA_KV_W
        return proj3[:, :, lo:lo + NSA_KV_W].reshape(b, s, g, d)

    k_cmp = _compress(kv(0), pos_k, ck1, ck2)
    v_cmp = _compress(kv(1), pos_v, cv1, cv2)
    to_t = lambda t: t.transpose(0, 2, 3, 1).astype(BF16)
    to_r = lambda t: t.transpose(0, 2, 1, 3).astype(BF16)
    gl = proj3[:, :, NSA_Q_W + 6 * NSA_KV_W:n_in].reshape(b, s, g, NSA_HPG, 3)
    gl_t = gl.transpose(0, 2, 4, 3, 1).reshape(b, g, 3 * NSA_HPG, s)
    gl_t = jnp.pad(gl_t.astype(F32), ((0, 0), (0, 0), (0, 16 - 3 * NSA_HPG), (0, 0)))
    o = _nsa_attention(proj3, gl_t, k_cmp.transpose(0, 1, 3, 2).astype(BF16), v_cmp.astype(BF16),
                       to_t(kv(2)), to_r(kv(3)), to_t(kv(4)), to_r(kv(5)))
    return _mm(o.reshape(n, -1), w_out, epi='residual', epi_args=(x, gate), rows_per_batch=s,
               name='odd_out_proj')


def _moe(x, mod, norm_g, w_router, b_router, w1, w3, w2, b, s):
    shift, scale, gate = mod
    n = b * s
    h, logits = _norm_router(x, norm_g, scale, shift, w_router, s)
    logits = logits[:, :N_EXPERTS] + b_router.astype(F32)
    top_val, top_idx = lax.top_k(logits, TOP_K)
    gate_w = jax.nn.softmax(top_val, axis=-1)
    flat_e = top_idx.reshape(-1)
    onehot = (flat_e[:, None] == jnp.arange(N_EXPERTS, dtype=jnp.int32)[None, :]).astype(jnp.int32)
    rank = jnp.take_along_axis(jnp.cumsum(onehot, axis=0), flat_e[:, None], axis=1)[:, 0] - 1
    counts = jnp.sum(onehot, axis=0)
    padded = ((counts + MOE_BLOCK - 1) // MOE_BLOCK) * MOE_BLOCK
    pad_end = jnp.cumsum(padded)
    pad_start = pad_end - padded
    dest = pad_start[flat_e] + rank
    n_assign = n * TOP_K
    n_blk = -(-n_assign // MOE_BLOCK) + N_EXPERTS
    n_rows = n_blk * MOE_BLOCK
    row_assign = jnp.zeros((n_rows,), jnp.int32).at[dest].set(
        jnp.arange(n_assign, dtype=jnp.int32), unique_indices=True)
    row_tok = row_assign // TOP_K
    blk_exp = jnp.minimum(jnp.searchsorted(pad_end, jnp.arange(n_blk, dtype=jnp.int32) * MOE_BLOCK,
                                           side='right'), N_EXPERTS - 1).astype(jnp.int32)
    n_used = (pad_end[-1:] // MOE_BLOCK).astype(jnp.int32)
    row_w = gate_w.reshape(-1)[row_assign]
    y_rows = _swiglu_grouped(h[row_tok], row_w[:, None], blk_exp, n_used, w1, w3, w2)
    return _moe_combine(x, y_rows[dest].reshape(n, -1), gate, s)


def kernel(x, c, positions, ada_w, ada_b, norm_g, final_g, ev_w_in, ev_conv_w, ev_a_log, ev_dt_bias, ev_dn_norm, ev_q_norm, ev_kv_norm, ev_w_uq, ev_w_ukv, ev_w_out, ev_ff_gate, ev_ff_up, ev_ff_down, od_w_in, od_cmp_pos_k, od_cmp_pos_v, od_cmp_k1, od_cmp_k2, od_cmp_v1, od_cmp_v2, od_w_out, od_router, od_router_b, od_moe_w1, od_moe_w3, od_moe_w2):
    b, s, dm = x.shape
    depth = ada_w.shape[0]
    inv = 1.0 / (ROPE_THETA ** (jnp.arange(0, MLA_ROPE, 2, dtype=F32) / MLA_ROPE))
    ang = positions.astype(F32)[..., None] * inv
    cos, sin = jnp.cos(ang), jnp.sin(ang)
    mods = jnp.einsum('bd,lkde->lkbe', jax.nn.silu(c), ada_w,
                      precision=lax.Precision.HIGHEST) + ada_b[:, :, None, :]

    def mod(layer, k):
        m = mods[layer, k]
        return tuple(m[:, None, i * dm:(i + 1) * dm] for i in range(3))

    xf = x.reshape(b * s, dm)
    for layer in range(depth):
        j = layer // 2
        if layer % 2 == 0:
            xf = _even_token_mixer(xf, mod(layer, 0), norm_g[layer, 0], cos, sin, ev_w_in[j],
                                   ev_conv_w[j], ev_a_log[j], ev_dt_bias[j], ev_dn_norm[j],
                                   ev_q_norm[j], ev_kv_norm[j], ev_w_uq[j], ev_w_ukv[j],
                                   ev_w_out[j], b, s)
            shift, scale, gate = mod(layer, 1)
            xf = _swiglu_dense(xf, norm_g[layer, 1], scale, shift, ev_ff_gate[j], ev_ff_up[j],
                               ev_ff_down[j], gate, s)
        else:
            xf = _odd_token_mixer(xf, mod(layer, 0), norm_g[layer, 0], od_w_in[j],
                                  od_cmp_pos_k[j], od_cmp_pos_v[j], od_cmp_k1[j], od_cmp_k2[j],
                                  od_cmp_v1[j], od_cmp_v2[j], od_w_out[j], b, s)
            xf = _moe(xf, mod(layer, 1), norm_g[layer, 1], od_router[j], od_router_b[j],
                      od_moe_w1[j], od_moe_w3[j], od_moe_w2[j], b, s)
    return _final_norm(xf, final_g).reshape(b, s, dm)
```

```python
import functools
import math

import jax
import jax.numpy as jnp
import numpy as np
from jax import lax
from jax.experimental import pallas as pl
from jax.experimental.pallas import tpu as pltpu

F32 = jnp.float32
BF16 = jnp.bfloat16

NORM_EPS = 1e-6
NEG_INF = -1e30
TINY = 1e-30

LANES = 128

DN_HEADS = 4
DN_HEAD_DIM = 128
DN_CONV = 4
DN_CHUNK = 64
DN_W = DN_HEADS * DN_HEAD_DIM

MLA_HEADS = 4
MLA_Q_RANK = 256
MLA_KV_RANK = 128
MLA_NOPE = 128
MLA_ROPE = 64
MLA_V = 128
MLA_QK = MLA_NOPE + MLA_ROPE
ROPE_THETA = 10000.0

NSA_HEADS = 16
NSA_GROUPS = 4
NSA_HPG = NSA_HEADS // NSA_GROUPS
NSA_HEAD_DIM = 64
NSA_Q_W = NSA_HEADS * NSA_HEAD_DIM
NSA_KV_W = NSA_GROUPS * NSA_HEAD_DIM
CMP_LEN = 32
CMP_STRIDE = 16
CMP_HIDDEN = 256
SLC_LEN = 64
SLC_TOPN = 16
WIN = 512
FORCE_SCORE = 1e9

N_EXPERTS = 8
TOP_K = 2
MOE_BLOCK = 512

VMEM_LIMIT = 56 * 1024 * 1024


def _params(sem):
    return pltpu.CompilerParams(dimension_semantics=sem, vmem_limit_bytes=VMEM_LIMIT)


def _sigmoid(x):
    return 1.0 / (1.0 + jnp.exp(-x))


def _silu(x):
    return x * _sigmoid(x)


def _dot(a, b):
    return jnp.dot(a.astype(BF16), b.astype(BF16), preferred_element_type=F32)


def _dot_nt(a, b):
    return lax.dot_general(a.astype(BF16), b.astype(BF16), (((1,), (1,)), ((), ())),
                           preferred_element_type=F32)


def _dot_tn(a, b):
    return lax.dot_general(a.astype(BF16), b.astype(BF16), (((0,), (0,)), ((), ())),
                           preferred_element_type=F32)


def _split3(x):
    hi = x.astype(BF16)
    r1 = x - hi.astype(F32)
    mid = r1.astype(BF16)
    lo = (r1 - mid.astype(F32)).astype(BF16)
    return hi, mid, lo


def _dot_lhs01(a01, x):
    hi, mid, lo = _split3(x)
    a = a01.astype(BF16)
    d = functools.partial(jnp.dot, preferred_element_type=F32)
    return d(a, hi) + d(a, mid) + d(a, lo)


def _dot_rhs01(x, b01):
    hi, mid, lo = _split3(x)
    b = b01.astype(BF16)
    d = functools.partial(jnp.dot, preferred_element_type=F32)
    return d(hi, b) + d(mid, b) + d(lo, b)


def _dot_hi(a, b):
    ah = a.astype(BF16)
    al = (a - ah.astype(F32)).astype(BF16)
    bh = b.astype(BF16)
    bl = (b - bh.astype(F32)).astype(BF16)
    d = functools.partial(jnp.dot, preferred_element_type=F32)
    return d(ah, bh) + d(ah, bl) + d(al, bh)


def _rms(x, gain):
    return x * lax.rsqrt(jnp.mean(x * x, axis=-1, keepdims=True) + NORM_EPS) * gain


def _mm_body(*refs, pro, epi):
    x_ref, w_ref = refs[0], refs[1]
    pos = 2
    if pro == 'adaln':
        g_ref, sc_ref, sh_ref = refs[pos:pos + 3]
        pos += 3
    elif pro == 'rms':
        g_ref = refs[pos]
        pos += 1
    if epi == 'residual':
        res_ref, gate_ref = refs[pos:pos + 2]
        pos += 2
    o_ref = refs[pos]
    pos += 1
    if pro is not None:
        h_ref = refs[pos]

        @pl.when(pl.program_id(1) == 0)
        def _():
            y = _rms(x_ref[...].astype(F32), g_ref[...])
            if pro == 'adaln':
                y = y * (1.0 + sc_ref[...]) + sh_ref[...]
            h_ref[...] = y.astype(BF16)

        h = h_ref[...]
    else:
        h = x_ref[...].astype(BF16)
    acc = jnp.dot(h, w_ref[...], preferred_element_type=F32)
    if epi == 'residual':
        acc = res_ref[...] + gate_ref[...] * acc
    o_ref[...] = acc.astype(o_ref.dtype)


def _pick_tile(n, cap):
    best = LANES
    t = LANES
    while t <= min(n, cap):
        if n % t == 0:
            best = t
        t += LANES
    return best


def _mm(x, w, *, pro=None, pro_args=(), epi=None, epi_args=(), rows_per_batch=None,
        out_dtype=F32, tm=512, tn_cap=1024, name='mm'):
    m, k = x.shape
    n = w.shape[1]
    tm = min(tm, m)
    tn = _pick_tile(n, tn_cap)
    assert m % tm == 0 and n % tn == 0
    rpb = None if rows_per_batch is None else rows_per_batch // tm
    in_specs = [pl.BlockSpec((tm, k), lambda i, j: (i, 0)),
                pl.BlockSpec((k, tn), lambda i, j: (0, j))]
    args = [x, w.astype(BF16)]
    scratch = []
    if pro == 'adaln':
        g, sc, sh = pro_args
        in_specs += [pl.BlockSpec((1, k), lambda i, j: (0, 0)),
                     pl.BlockSpec((None, 1, k), lambda i, j: (i // rpb, 0, 0)),
                     pl.BlockSpec((None, 1, k), lambda i, j: (i // rpb, 0, 0))]
        args += [g.reshape(1, k), sc, sh]
    elif pro == 'rms':
        in_specs += [pl.BlockSpec((1, k), lambda i, j: (0, 0))]
        args += [pro_args[0].reshape(1, k)]
    if pro is not None:
        scratch = [pltpu.VMEM((tm, k), BF16)]
    if epi == 'residual':
        res, gate = epi_args
        in_specs += [pl.BlockSpec((tm, tn), lambda i, j: (i, j)),
                     pl.BlockSpec((None, 1, tn), lambda i, j: (i // rpb, 0, j))]
        args += [res, gate]
    return pl.pallas_call(
        functools.partial(_mm_body, pro=pro, epi=epi),
        out_shape=jax.ShapeDtypeStruct((m, n), out_dtype),
        grid=(m // tm, n // tn),
        in_specs=in_specs,
        out_specs=pl.BlockSpec((tm, tn), lambda i, j: (i, j)),
        scratch_shapes=scratch,
        compiler_params=_params(("parallel", "arbitrary")),
        name=name,
    )(*args)


def _swiglu_dense_body(x_ref, g_ref, sc_ref, sh_ref, wg_ref, wu_ref, wd_ref, gate_ref, o_ref,
                       h_ref, acc_ref):
    j = pl.program_id(1)

    @pl.when(j == 0)
    def _():
        y = _rms(x_ref[...], g_ref[...]) * (1.0 + sc_ref[...]) + sh_ref[...]
        h_ref[...] = y.astype(BF16)
        acc_ref[...] = jnp.zeros_like(acc_ref)

    h = h_ref[...]
    a = jnp.dot(h, wg_ref[...], preferred_element_type=F32)
    b = jnp.dot(h, wu_ref[...], preferred_element_type=F32)
    act = (_silu(a) * b).astype(BF16)
    acc_ref[...] += jnp.dot(act, wd_ref[...], preferred_element_type=F32)

    @pl.when(j == pl.num_programs(1) - 1)
    def _():
        o_ref[...] = x_ref[...] + gate_ref[...] * acc_ref[...]


def _swiglu_dense(x, g, sc, sh, wg, wu, wd, gate, rows_per_batch, *, tm=1024, tf_cap=256):
    m, k = x.shape
    f = wg.shape[1]
    tf = _pick_tile(f, tf_cap)
    assert m % tm == 0 and f % tf == 0
    rpb = rows_per_batch // tm
    return pl.pallas_call(
        _swiglu_dense_body,
        out_shape=jax.ShapeDtypeStruct((m, k), F32),
        grid=(m // tm, f // tf),
        in_specs=[pl.BlockSpec((tm, k), lambda i, j: (i, 0)),
                  pl.BlockSpec((1, k), lambda i, j: (0, 0)),
                  pl.BlockSpec((None, 1, k), lambda i, j: (i // rpb, 0, 0)),
                  pl.BlockSpec((None, 1, k), lambda i, j: (i // rpb, 0, 0)),
                  pl.BlockSpec((k, tf), lambda i, j: (0, j)),
                  pl.BlockSpec((k, tf), lambda i, j: (0, j)),
                  pl.BlockSpec((tf, k), lambda i, j: (j, 0)),
                  pl.BlockSpec((None, 1, k), lambda i, j: (i // rpb, 0, 0))],
        out_specs=pl.BlockSpec((tm, k), lambda i, j: (i, 0)),
        scratch_shapes=[pltpu.VMEM((tm, k), BF16), pltpu.VMEM((tm, k), F32)],
        compiler_params=_params(("parallel", "arbitrary")),
        name='swiglu_dense',
    )(x, g.reshape(1, k), sc, sh, wg.astype(BF16), wu.astype(BF16), wd.astype(BF16), gate)


def _swiglu_grouped_body(be_ref, nu_ref, x_ref, rw_ref, wg_ref, wu_ref, wd_ref, o_ref, acc_ref):
    i = pl.program_id(0)
    j = pl.program_id(1)
    used = i < nu_ref[0]

    @pl.when(j == 0)
    def _():
        acc_ref[...] = jnp.zeros_like(acc_ref)

    @pl.when(used)
    def _():
        h = x_ref[...]
        a = jnp.dot(h, wg_ref[...], preferred_element_type=F32)
        b = jnp.dot(h, wu_ref[...], preferred_element_type=F32)
        act = (_silu(a) * b).astype(BF16)
        acc_ref[...] += jnp.dot(act, wd_ref[...], preferred_element_type=F32)

    @pl.when(j == pl.num_programs(1) - 1)
    def _():
        o_ref[...] = (acc_ref[...] * rw_ref[...]).astype(o_ref.dtype)


def _swiglu_grouped(x_rows, row_w, blk_exp, n_used, w1, w3, w2, layer, *, tm=MOE_BLOCK,
                    tf_cap=1792):
    n_rows, k = x_rows.shape
    f = w1.shape[3]
    tf = _pick_tile(f, tf_cap)
    assert n_rows % tm == 0 and f % tf == 0
    nj = f // tf

    def jj(i, j, nu):
        return jnp.where(i < nu[0], j, nj - 1)

    grid_spec = pltpu.PrefetchScalarGridSpec(
        num_scalar_prefetch=2,
        grid=(n_rows // tm, nj),
        in_specs=[pl.BlockSpec((tm, k), lambda i, j, be, nu: (i, 0)),
                  pl.BlockSpec((tm, 1), lambda i, j, be, nu: (i, 0)),
                  pl.BlockSpec((None, None, k, tf),
                               lambda i, j, be, nu: (layer, be[i], 0, jj(i, j, nu))),
                  pl.BlockSpec((None, None, k, tf),
                               lambda i, j, be, nu: (layer, be[i], 0, jj(i, j, nu))),
                  pl.BlockSpec((None, None, tf, k),
                               lambda i, j, be, nu: (layer, be[i], jj(i, j, nu), 0))],
        out_specs=pl.BlockSpec((tm, k), lambda i, j, be, nu: (i, 0)),
        scratch_shapes=[pltpu.VMEM((tm, k), F32)],
    )
    return pl.pallas_call(
        _swiglu_grouped_body,
        out_shape=jax.ShapeDtypeStruct((n_rows, k), BF16),
        grid_spec=grid_spec,
        compiler_params=_params(("arbitrary", "arbitrary")),
        name='swiglu_grouped',
    )(blk_exp, n_used, x_rows, row_w, w1.astype(BF16), w3.astype(BF16), w2.astype(BF16))


def _moe_combine_body(x_ref, gate_ref, *refs):
    y_refs, o_ref = refs[:-1], refs[-1]
    total = y_refs[0][...].astype(F32)
    for y_ref in y_refs[1:]:
        total = total + y_ref[...].astype(F32)
    o_ref[...] = x_ref[...] + gate_ref[...] * total


def _moe_combine(x, y_slots, gate, rows_per_batch, *, tm=512):
    m, k = x.shape
    rpb = rows_per_batch // tm
    row_blk = pl.BlockSpec((tm, k), lambda i: (i, 0))
    return pl.pallas_call(
        _moe_combine_body,
        out_shape=jax.ShapeDtypeStruct((m, k), F32),
        grid=(m // tm,),
        in_specs=[row_blk, pl.BlockSpec((None, 1, k), lambda i: (i // rpb, 0, 0))]
                 + [row_blk] * len(y_slots),
        out_specs=row_blk,
        compiler_params=_params(("parallel",)),
        name='moe_combine',
    )(x, gate, *y_slots)


def _norm_router_body(x_ref, g_ref, sc_ref, sh_ref, wr_ref, h_ref, lg_ref):
    y = _rms(x_ref[...], g_ref[...]) * (1.0 + sc_ref[...]) + sh_ref[...]
    h_ref[...] = y.astype(BF16)
    lg_ref[...] = jnp.dot(y, wr_ref[...], preferred_element_type=F32,
                          precision=lax.Precision.HIGHEST)


def _norm_router(x, g, sc, sh, w_router, rows_per_batch, *, tm=512):
    m, k = x.shape
    e = w_router.shape[1]
    wr = jnp.zeros((k, LANES), F32).at[:, :e].set(w_router)
    rpb = rows_per_batch // tm
    return pl.pallas_call(
        _norm_router_body,
        out_shape=(jax.ShapeDtypeStruct((m, k), BF16), jax.ShapeDtypeStruct((m, LANES), F32)),
        grid=(m // tm,),
        in_specs=[pl.BlockSpec((tm, k), lambda i: (i, 0)),
                  pl.BlockSpec((1, k), lambda i: (0, 0)),
                  pl.BlockSpec((None, 1, k), lambda i: (i // rpb, 0, 0)),
                  pl.BlockSpec((None, 1, k), lambda i: (i // rpb, 0, 0)),
                  pl.BlockSpec((k, LANES), lambda i: (0, 0))],
        out_specs=(pl.BlockSpec((tm, k), lambda i: (i, 0)),
                   pl.BlockSpec((tm, LANES), lambda i: (i, 0))),
        compiler_params=_params(("parallel",)),
        name='norm_router',
    )(x, g.reshape(1, k), sc, sh, wr)


def _final_norm_body(x_ref, g_ref, o_ref):
    o_ref[...] = _rms(x_ref[...], g_ref[...])


def _final_norm(x, g, *, tm=1024):
    m, k = x.shape
    return pl.pallas_call(
        _final_norm_body,
        out_shape=jax.ShapeDtypeStruct((m, k), F32),
        grid=(m // tm,),
        in_specs=[pl.BlockSpec((tm, k), lambda i: (i, 0)), pl.BlockSpec((1, k), lambda i: (0, 0))],
        out_specs=pl.BlockSpec((tm, k), lambda i: (i, 0)),
        compiler_params=_params(("parallel",)),
        name='final_norm',
    )(x, g.reshape(1, k))


def _softplus(x):
    return jnp.maximum(x, 0.0) + jnp.log(1.0 + jnp.exp(-jnp.abs(x)))


CONV_HALO = 8


def _deltanet_body(q_ref, k_ref, v_ref, z_ref, pb_ref, pa_ref, cwq_ref, cwk_ref, cwv_ref, alog_ref,
                   dtb_ref, gn_ref, o_ref, s_ref, xs_ref, *, n_chunks):
    c_len = DN_CHUNK
    tile = n_chunks * c_len

    @pl.when(pl.program_id(1) == 0)
    def _():
        s_ref[...] = jnp.zeros_like(s_ref)
        xs_ref[:, 0:CONV_HALO, :] = jnp.zeros((3, CONV_HALO, xs_ref.shape[-1]), F32)

    def conv(i, x_ref, cw_ref):
        xs_ref[i, CONV_HALO:CONV_HALO + tile, :] = x_ref[...]
        y = None
        for tap in range(DN_CONV):
            off = CONV_HALO - (DN_CONV - 1) + tap
            term = xs_ref[i, off:off + tile, :] * cw_ref[tap:tap + 1, :]
            y = term if y is None else y + term
        xs_ref[i, 0:CONV_HALO, :] = xs_ref[i, tile:tile + CONV_HALO, :]
        return _silu(y)

    q = conv(0, q_ref, cwq_ref)
    k = conv(1, k_ref, cwk_ref)
    v = conv(2, v_ref, cwv_ref)

    row = lax.broadcasted_iota(jnp.int32, (c_len, c_len), 0)
    col = lax.broadcasted_iota(jnp.int32, (c_len, c_len), 1)
    incl = row >= col
    strict = row > col
    incl_f = incl.astype(F32)
    strict_f = strict.astype(F32)
    eye = (row == col).astype(F32)
    q = q * lax.rsqrt(jnp.sum(q * q, axis=-1, keepdims=True) + NORM_EPS) * (DN_HEAD_DIM ** -0.5)
    k = k * lax.rsqrt(jnp.sum(k * k, axis=-1, keepdims=True) + NORM_EPS)
    beta = _sigmoid(pb_ref[...])
    g = -jnp.exp(alog_ref[...]) * _softplus(pa_ref[...] + dtb_ref[...])
    kb = k * beta
    vb = v * beta
    chunks = range(n_chunks)
    cs = lambda x, c: x[c * c_len:(c + 1) * c_len]
    gc = [_dot_lhs01(incl_f, cs(g, c)) for c in chunks]
    gdiff = [_dot_lhs01(incl_f, cs(g, c)[:, :c_len] * strict_f) for c in chunks]
    decay = [jnp.where(incl, jnp.exp(jnp.where(incl, gd, 0.0)), 0.0) for gd in gdiff]
    eg = [jnp.exp(x) for x in gc]
    lower = [jnp.where(strict, _dot_nt(cs(kb, c), cs(k, c)) * decay[c], 0.0) for c in chunks]
    tinv = [eye - lo for lo in lower]
    pw = lower
    for it in range(5):
        mul = _dot_hi if it == 0 else _dot
        pw = [mul(p, p) for p in pw]
        tinv = [t + mul(t, p) for t, p in zip(tinv, pw)]
    u = [_dot(tinv[c], cs(vb, c)) for c in chunks]
    w = [_dot(tinv[c], cs(kb, c) * eg[c]) for c in chunks]
    qk = [_dot_nt(cs(q, c), cs(k, c)) * decay[c] for c in chunks]
    g_last = [x[c_len - 1:c_len, :] for x in gc]
    k_tail = [cs(k, c) * jnp.exp(g_last[c] - gc[c]) for c in chunks]
    m_mat = [_dot_tn(k_tail[c], w[c]) for c in chunks]
    b_mat = [_dot_tn(k_tail[c], u[c]) for c in chunks]
    q_eff = [cs(q, c) * eg[c] - _dot(qk[c], w[c]) for c in chunks]
    o_loc = [_dot(qk[c], u[c]) for c in chunks]
    state = s_ref[...]
    outs = []
    for c in chunks:
        outs.append(_dot(q_eff[c], state) + o_loc[c])
        state = state * jnp.exp(g_last[c]) - _dot(m_mat[c], state) + b_mat[c]
    s_ref[...] = state
    o = jnp.concatenate(outs, axis=0)
    z = z_ref[...]
    o_ref[...] = (_rms(o, gn_ref[...]) * _silu(z)).astype(o_ref.dtype)


def _deltanet(proj, conv_w, a_log, dt_bias, dn_norm, *, z_blk, pb_blk, pa_blk, tile=512):
    b, s, _ = proj.shape
    h = DN_HEADS
    d = DN_HEAD_DIM
    tile = min(tile, s)
    assert s % tile == 0 and tile % DN_CHUNK == 0
    rep = lambda t: jnp.broadcast_to(t.astype(F32)[:, None, None], (h, 1, d))

    def col(off):
        return pl.BlockSpec((None, tile, d), lambda bh, t, off=off: (bh // h, t, off + bh % h))

    def cw(off):
        return pl.BlockSpec((DN_CONV, d), lambda bh, t, off=off: (0, off + bh % h))

    per_head = pl.BlockSpec((None, 1, d), lambda bh, t: (bh % h, 0, 0))
    return pl.pallas_call(
        functools.partial(_deltanet_body, n_chunks=tile // DN_CHUNK),
        out_shape=jax.ShapeDtypeStruct((b, s, h * d), BF16),
        grid=(b * h, s // tile),
        in_specs=[col(0), col(h), col(2 * h), col(z_blk), col(pb_blk), col(pa_blk),
                  cw(0), cw(h), cw(2 * h),
                  per_head, per_head, pl.BlockSpec((1, d), lambda bh, t: (0, 0))],
        out_specs=col(0),
        scratch_shapes=[pltpu.VMEM((d, d), F32), pltpu.VMEM((3, tile + CONV_HALO, d), F32)],
        compiler_params=_params(("parallel", "arbitrary")),
        name='deltanet',
    )(proj, proj, proj, proj, proj, proj, conv_w, conv_w, conv_w, rep(a_log), rep(dt_bias),
      dn_norm.reshape(1, d))


LOG2E = 1.4426950408889634


def _rope_halves(x1, x2, cos, sin):
    return x1 * cos - x2 * sin, x2 * cos + x1 * sin


def _mla_q_up_body(cq_ref, g_ref, w_ref, cos_ref, sin_ref, o_ref):
    hm, half = MLA_HEADS, MLA_ROPE // 2
    x = _rms(cq_ref[...], g_ref[...]).astype(BF16)
    acc = jnp.dot(x, w_ref[...], preferred_element_type=F32)
    nope_w = hm * MLA_NOPE
    r1, r2 = _rope_halves(acc[:, nope_w:nope_w + hm * half], acc[:, nope_w + hm * half:],
                          cos_ref[...], sin_ref[...])
    for h in range(hm):
        q_h = jnp.concatenate([acc[:, h * MLA_NOPE:(h + 1) * MLA_NOPE],
                               r1[:, h * half:(h + 1) * half], r2[:, h * half:(h + 1) * half]], axis=1)
        o_ref[h] = (q_h * (MLA_QK ** -0.5 * LOG2E)).astype(o_ref.dtype)


def _mla_q_up(proj, q_norm, w_uq, cos_t, sin_t, b, s, *, cq_blk, tm=512):
    n = proj.shape[0]
    hm, half = MLA_HEADS, MLA_ROPE // 2
    assert hm * half == LANES and s % tm == 0
    heads = np.arange(hm)[:, None] * MLA_QK
    cols = np.concatenate([(heads + np.arange(MLA_NOPE)[None, :]).ravel(),
                           (heads + MLA_NOPE + np.arange(half)[None, :]).ravel(),
                           (heads + MLA_NOPE + half + np.arange(half)[None, :]).ravel()])
    rpb = s // tm
    return pl.pallas_call(
        _mla_q_up_body,
        out_shape=jax.ShapeDtypeStruct((b, hm, s, MLA_QK), BF16),
        grid=(n // tm,),
        in_specs=[pl.BlockSpec((tm, MLA_Q_RANK), lambda i: (i, cq_blk)),
                  pl.BlockSpec((1, MLA_Q_RANK), lambda i: (0, 0)),
                  pl.BlockSpec((MLA_Q_RANK, hm * MLA_QK), lambda i: (0, 0)),
                  pl.BlockSpec((tm, LANES), lambda i: (i, 0)),
                  pl.BlockSpec((tm, LANES), lambda i: (i, 0))],
        out_specs=pl.BlockSpec((None, hm, tm, MLA_QK), lambda i: (i // rpb, 0, i % rpb, 0)),
        compiler_params=_params(("parallel",)),
        name='mla_q_up',
    )(proj, q_norm.reshape(1, -1), w_uq[:, cols].astype(BF16), cos_t, sin_t)


def _mla_kv_up_body(ckv_ref, kr_ref, g_ref, w_ref, cos_ref, sin_ref, k_ref, v_ref):
    hm, half = MLA_HEADS, MLA_ROPE // 2
    x = _rms(ckv_ref[...], g_ref[...]).astype(BF16)
    acc = jnp.dot(x, w_ref[...], preferred_element_type=F32)
    kr = kr_ref[...]
    r1, r2 = _rope_halves(kr[:, :half], kr[:, half:2 * half], cos_ref[:, :half], sin_ref[:, :half])
    for h in range(hm):
        k_h = jnp.concatenate([acc[:, h * MLA_NOPE:(h + 1) * MLA_NOPE], r1, r2], axis=1)
        k_ref[h] = k_h.astype(k_ref.dtype)
        v_lo = hm * MLA_NOPE + h * MLA_V
        v_ref[h] = acc[:, v_lo:v_lo + MLA_V].astype(v_ref.dtype)


def _mla_kv_up(proj, kv_norm, w_ukv, cos_t, sin_t, b, s, *, ckv_blk, kr_blk, tm=512):
    n = proj.shape[0]
    hm = MLA_HEADS
    assert MLA_KV_RANK == LANES and MLA_V == LANES and s % tm == 0
    heads = np.arange(hm)[:, None] * (MLA_NOPE + MLA_V)
    cols = np.concatenate([(heads + np.arange(MLA_NOPE)[None, :]).ravel(),
                           (heads + MLA_NOPE + np.arange(MLA_V)[None, :]).ravel()])
    rpb = s // tm
    head_major = lambda w: pl.BlockSpec((None, hm, tm, w), lambda i: (i // rpb, 0, i % rpb, 0))
    return pl.pallas_call(
        _mla_kv_up_body,
        out_shape=(jax.ShapeDtypeStruct((b, hm, s, MLA_QK), BF16),
                   jax.ShapeDtypeStruct((b, hm, s, MLA_V), BF16)),
        grid=(n // tm,),
        in_specs=[pl.BlockSpec((tm, LANES), lambda i: (i, ckv_blk)),
                  pl.BlockSpec((tm, LANES), lambda i: (i, kr_blk)),
                  pl.BlockSpec((1, LANES), lambda i: (0, 0)),
                  pl.BlockSpec((MLA_KV_RANK, hm * (MLA_NOPE + MLA_V)), lambda i: (0, 0)),
                  pl.BlockSpec((tm, LANES), lambda i: (i, 0)),
                  pl.BlockSpec((tm, LANES), lambda i: (i, 0))],
        out_specs=(head_major(MLA_QK), head_major(MLA_V)),
        compiler_params=_params(("parallel",)),
        name='mla_kv_up',
    )(proj, proj, kv_norm.reshape(1, -1), w_ukv[:, cols].astype(BF16), cos_t, sin_t)


def _mla_body(q_ref, k_ref, v_ref, o_ref, *, tq, tk):
    qi = pl.program_id(2)
    q = q_ref[...]
    dv = o_ref.shape[-1]

    def step(kt, carry, masked):
        m, l, acc = carry
        ks = pl.multiple_of(kt * tk, tk)
        k = k_ref[pl.ds(ks, tk), :]
        v = v_ref[pl.ds(ks, tk), :]
        s = lax.dot_general(q, k, (((1,), (1,)), ((), ())), preferred_element_type=F32)
        if masked:
            tpos = lax.broadcasted_iota(jnp.int32, (tq, tk), 0)
            kpos = lax.broadcasted_iota(jnp.int32, (tq, tk), 1)
            s = jnp.where(kpos <= tpos, s, NEG_INF)
        m_new = jnp.maximum(m, jnp.max(s, axis=-1, keepdims=True))
        alpha = jnp.exp2(m - m_new)
        p = jnp.exp2(s - m_new)
        l = alpha * l + jnp.sum(p, axis=-1, keepdims=True)
        acc = alpha * acc + jnp.dot(p.astype(BF16), v, preferred_element_type=F32)
        return m_new, l, acc

    init = (jnp.full((tq, 1), NEG_INF, F32), jnp.zeros((tq, 1), F32), jnp.zeros((tq, dv), F32))
    carry = lax.fori_loop(0, qi, lambda kt, c: step(kt, c, False), init)
    _, l, acc = step(qi, carry, True)
    o_ref[...] = (acc / l).astype(o_ref.dtype)


def _mla_attention(q, k, v, *, tq=512, tk=512):
    b, h, s, dqk = q.shape
    dva = v.shape[-1]
    dv = dva
    tq = min(tq, s)
    tk = min(tk, tq)
    assert s % tq == 0 and tq == tk
    return pl.pallas_call(
        functools.partial(_mla_body, tq=tq, tk=tk),
        out_shape=jax.ShapeDtypeStruct((b, s, h * dv), BF16),
        grid=(b, h, s // tq),
        in_specs=[pl.BlockSpec((None, None, tq, dqk), lambda bi, hi, qi: (bi, hi, qi, 0)),
                  pl.BlockSpec((None, None, s, dqk), lambda bi, hi, qi: (bi, hi, 0, 0)),
                  pl.BlockSpec((None, None, s, dva), lambda bi, hi, qi: (bi, hi, 0, 0))],
        out_specs=pl.BlockSpec((None, tq, dv), lambda bi, hi, qi: (bi, qi, hi)),
        compiler_params=_params(("parallel", "parallel", "arbitrary")),
        name='mla_attention',
    )(q, k, v)


SUBLANES = 8


def _compress_body(x_ref, pos_ref, w1_ref, w2_ref, o_ref, sh_ref):
    n, half = x_ref.shape
    x = x_ref[...].astype(F32)
    first = jnp.dot((x + pos_ref[0:1, :]).astype(BF16), w1_ref[0:half, :],
                    preferred_element_type=F32)
    second = jnp.dot((x + pos_ref[1:2, :]).astype(BF16), w1_ref[half:2 * half, :],
                     preferred_element_type=F32)
    sh_ref[0:n, :] = second
    sh_ref[n:n + SUBLANES, :] = jnp.zeros((SUBLANES, sh_ref.shape[-1]), F32)
    hid = _silu(first + sh_ref[1:n + 1, :])
    o_ref[...] = jnp.dot(hid.astype(BF16), w2_ref[...], preferred_element_type=F32)


def _compress(t, pos_emb, w1, w2):
    assert CMP_LEN == 2 * CMP_STRIDE
    b, s, g, d = t.shape
    n_chunk = s // CMP_STRIDE
    half = CMP_STRIDE * d
    ch = t.reshape(b, n_chunk, CMP_STRIDE, g, d).transpose(0, 3, 1, 2, 4).reshape(b, g, n_chunk, half)
    w2p = jnp.zeros((CMP_HIDDEN, LANES), BF16).at[:, :d].set(w2.astype(BF16))
    out = pl.pallas_call(
        _compress_body,
        out_shape=jax.ShapeDtypeStruct((b, g, n_chunk, LANES), F32),
        grid=(b, g),
        in_specs=[pl.BlockSpec((None, None, n_chunk, half), lambda bi, gi: (bi, gi, 0, 0)),
                  pl.BlockSpec((2, half), lambda bi, gi: (0, 0)),
                  pl.BlockSpec((2 * half, CMP_HIDDEN), lambda bi, gi: (0, 0)),
                  pl.BlockSpec((CMP_HIDDEN, LANES), lambda bi, gi: (0, 0))],
        out_specs=pl.BlockSpec((None, None, n_chunk, LANES), lambda bi, gi: (bi, gi, 0, 0)),
        scratch_shapes=[pltpu.VMEM((n_chunk + SUBLANES, CMP_HIDDEN), F32)],
        compiler_params=_params(("parallel", "parallel")),
        name='nsa_compress',
    )(ch, pos_emb.reshape(2, half), w1.astype(BF16), w2p)
    return out[..., :d]


MASK_BIG = 1e30


def _nsa_body(q_ref, gl_ref, kct_ref, vc_ref, kse_ref, vs_ref, kwt_ref, vw_ref, o_ref, *,
              tq, tk, n_blocks, n_top, span):
    hg = NSA_HPG
    d = NSA_HEAD_DIM
    rows = hg * tq
    qi = pl.program_id(2)
    q0 = qi * tq
    qf = q_ref[...].astype(F32) * (d ** -0.5 * LOG2E)
    q = jnp.concatenate([qf[:, h * d:(h + 1) * d] for h in range(hg)], axis=0).astype(BF16)
    t_q = q0 + lax.broadcasted_iota(jnp.int32, (tq, 1), 0)

    def add_bias(s, bias):
        n = s.shape[-1]
        return (s.reshape(hg, tq, n) + bias[None]).reshape(rows, n)


    n_cmp = kct_ref.shape[-1]
    cmp_end = lax.broadcasted_iota(jnp.int32, (1, n_cmp), 1) * CMP_STRIDE + (CMP_LEN - 1)
    bias_c = jnp.where(cmp_end <= t_q, 0.0, NEG_INF)
    s_c = add_bias(jnp.dot(q, kct_ref[...], preferred_element_type=F32), bias_c)
    e_c =jnp.exp2(s_c - jnp.max(s_c, axis=-1, keepdims=True))
    acc_c = jnp.dot(e_c.astype(BF16), vc_ref[...], preferred_element_type=F32)
    has_c = jnp.concatenate([t_q >= CMP_LEN - 1] * hg, axis=0)
    inv_c = jnp.where(has_c, 1.0 / jnp.maximum(acc_c[:, d:d + 1], TINY), 0.0)

    p_sum = jnp.sum((e_c * inv_c).reshape(hg, tq, n_cmp), axis=0)
    jj = lax.broadcasted_iota(jnp.int32, (n_cmp, LANES), 0)
    nn = lax.broadcasted_iota(jnp.int32, (n_cmp, LANES), 1)
    per = SLC_LEN // CMP_STRIDE
    pool = 0.5 * ((jj // per == nn).astype(F32) + ((jj + 1) // per == nn).astype(F32))
    imp = _dot_rhs01(p_sum, pool)
    sel_parts = []
    blk = lax.broadcasted_iota(jnp.int32, (LANES, LANES), 0)
    blk_f = blk.astype(F32)
    for r in range(tq // LANES):
        imp_t = imp[r * LANES:(r + 1) * LANES].T
        cur = (q0 + r * LANES + lax.broadcasted_iota(jnp.int32, (1, LANES), 1)) // SLC_LEN
        forced = (blk == 0) | (blk == cur) | (blk == cur - 1)
        causal = blk <= cur
        imp_t = jnp.where(causal & jnp.logical_not(forced), imp_t, -2.0)
        sel_t = jnp.zeros((LANES, LANES), F32)
        for _ in range(n_top - 3):
            top = jnp.max(imp_t, axis=0, keepdims=True)
            first = jnp.min(jnp.where(imp_t == top, blk_f, float(LANES)), axis=0, keepdims=True)
            hit = blk_f == first
            sel_t = jnp.where(hit, 1.0, sel_t)
            imp_t = jnp.where(hit, -2.0, imp_t)
        sel_t = jnp.where(cur >= n_top, jnp.where(forced, 1.0, sel_t), jnp.where(causal, 1.0, 0.0))
        sel_parts.append(sel_t.T)
    sel = jnp.concatenate(sel_parts, axis=0)
    unsel = jnp.concatenate([sel - 1.0] * hg, axis=0)
    q_aug = jnp.concatenate([unsel.astype(BF16), q], axis=1)
    t_rel = t_q - lax.broadcasted_iota(jnp.int32, (1, tk), 1)

    def sel_tile(kt, carry, causal):
        m, acc = carry
        ks = pl.multiple_of(kt * tk, tk)
        s = jnp.dot(q_aug, kse_ref[:, pl.ds(ks, tk)], preferred_element_type=F32)
        if causal:
            s = add_bias(s, jnp.where(t_rel >= ks, 0.0, NEG_INF))
        m_new = jnp.maximum(m, jnp.max(s, axis=-1, keepdims=True))
        p = jnp.exp2(s - m_new)
        acc = jnp.exp2(m - m_new) * acc + jnp.dot(p.astype(BF16), vs_ref[pl.ds(ks, tk), :],
                                                  preferred_element_type=F32)
        return m_new, acc

    init = (jnp.full((rows, 1), NEG_INF, F32), jnp.zeros((rows, vs_ref.shape[-1]), F32))
    n_kt = (q0 + tq + tk - 1) // tk
    carry = lax.fori_loop(0, n_kt - 1, lambda kt, c: sel_tile(kt, c, False), init)
    _, acc_s = sel_tile(n_kt - 1, carry, True)
    inv_s = 1.0 / jnp.maximum(acc_s[:, d:d + 1], TINY)

    ws = pl.multiple_of(jnp.maximum(q0 + tq - span, 0), LANES)
    dist = (t_q - ws) - lax.broadcasted_iota(jnp.int32, (1, span), 1)
    bias_w = jnp.where((dist >= 0) & (dist < WIN), 0.0, NEG_INF)
    s_w = add_bias(jnp.dot(q, kwt_ref[:, pl.ds(ws, span)], preferred_element_type=F32), bias_w)
    e_w = jnp.exp2(s_w - jnp.max(s_w, axis=-1, keepdims=True))
    acc_w = jnp.dot(e_w.astype(BF16), vw_ref[pl.ds(ws, span), :], preferred_element_type=F32)
    inv_w = 1.0 / acc_w[:, d:d + 1]

    gates = _sigmoid(gl_ref[...])
    g_t = []
    for r in range(tq // LANES):
        blk_g = jnp.concatenate([gates[:, r * LANES:(r + 1) * LANES],
                                 jnp.zeros((LANES - gates.shape[0], LANES), F32)], axis=0)
        g_t.append(blk_g.T)
    g_t = jnp.concatenate(g_t, axis=0)

    def gate_col(branch):
        return jnp.concatenate([g_t[:, branch * hg + h:branch * hg + h + 1] for h in range(hg)], axis=0)

    o = ((gate_col(0) * inv_c) * acc_c[:, :d] + (gate_col(1) * inv_s) * acc_s[:, :d]
         + (gate_col(2) * inv_w) * acc_w[:, :d])
    o_ref[...] = jnp.concatenate([o[h * tq:(h + 1) * tq] for h in range(hg)], axis=1).astype(o_ref.dtype)


def _nsa_attention(proj, gl_t, kct, vc, kst, vs, kwt, vw, *, tq=256, tk=512):
    b, s, _ = proj.shape
    g = NSA_GROUPS
    d = NSA_HEAD_DIM
    n_cmp = kct.shape[-1]
    tq = min(tq, s)
    tk = min(tk, s)
    span = min(WIN + tq, s)
    n_blocks = s // SLC_LEN
    assert s % tq == 0 and s % tk == 0 and tq % LANES == 0 and n_blocks <= LANES
    kv_t = lambda n, r=d: pl.BlockSpec((None, None, r, n), lambda bi, gi, qi: (bi, gi, 0, 0))
    kv_r = lambda n: pl.BlockSpec((None, None, n, LANES), lambda bi, gi, qi: (bi, gi, 0, 0))

    def with_ones(v):
        tail = jnp.zeros(v.shape[:-1] + (LANES - d,), BF16).at[..., 0].set(1.0)
        return jnp.concatenate([v, tail], axis=-1)

    blk_of_key = jnp.arange(s, dtype=jnp.int32) // SLC_LEN
    mask_rows = jnp.where(jnp.arange(LANES, dtype=jnp.int32)[:, None] == blk_of_key[None, :],
                          MASK_BIG, 0.0).astype(BF16)
    kse = jnp.concatenate([jnp.broadcast_to(mask_rows, (b, g, LANES, s)), kst], axis=2)
    vc, vs, vw = with_ones(vc), with_ones(vs), with_ones(vw)
    kst = kse
    return pl.pallas_call(
        functools.partial(_nsa_body, tq=tq, tk=tk, n_blocks=n_blocks,
                          n_top=min(SLC_TOPN, n_blocks), span=span),
        out_shape=jax.ShapeDtypeStruct((b, s, g * NSA_HPG * d), BF16),
        grid=(b, g, s // tq),
        in_specs=[pl.BlockSpec((None, tq, NSA_HPG * d), lambda bi, gi, qi: (bi, qi, gi)),
                  pl.BlockSpec((None, None, 16, tq), lambda bi, gi, qi: (bi, gi, 0, qi)),
                  kv_t(n_cmp), kv_r(n_cmp), kv_t(s, LANES + d), kv_r(s), kv_t(s), kv_r(s)],
        out_specs=pl.BlockSpec((None, tq, NSA_HPG * d), lambda bi, gi, qi: (bi, qi, gi)),
        compiler_params=_params(("parallel", "parallel", "arbitrary")),
        name='nsa_attention',
    )(proj, gl_t, kct, vc, kst, vs, kwt, vw)


def _pad_cols(w, n):
    return jnp.pad(w, ((0, 0), (0, n - w.shape[1])))


def _even_token_mixer(x, mod, norm_g, cos, sin, w_in, conv_w, a_log, dt_bias, dn_norm, q_norm,
                      kv_norm, w_uq, w_ukv, w_out, b, s):
    shift, scale, gate = mod
    n = b * s
    hd = DN_HEADS
    o_b = 4 * DN_W
    o_a = o_b + hd
    o_cq = o_a + hd
    o_ckv = o_cq + MLA_Q_RANK
    o_kr = o_ckv + MLA_KV_RANK
    w_cols = [w_in[:, :o_b], w_in[:, o_cq:o_ckv], w_in[:, o_ckv:o_kr],
              _pad_cols(w_in[:, o_kr:o_kr + MLA_ROPE], LANES),
              jnp.repeat(w_in[:, o_b:o_a], DN_HEAD_DIM, axis=1),
              jnp.repeat(w_in[:, o_a:o_cq], DN_HEAD_DIM, axis=1)]
    w_all = jnp.concatenate(w_cols, axis=1)
    proj = _mm(x, w_all, pro='adaln', pro_args=(norm_g, scale, shift), rows_per_batch=s,
               tn_cap=4096, name='even_in_proj')
    width = proj.shape[1]
    c0 = 4 * DN_W
    pb_blk = (c0 + MLA_Q_RANK + MLA_KV_RANK + LANES) // LANES
    proj3 = proj.reshape(b, s, width)

    o_dn = _deltanet(proj3, conv_w.astype(F32), a_log, dt_bias, dn_norm, z_blk=3 * hd,
                     pb_blk=pb_blk, pa_blk=pb_blk + hd)

    reps = LANES // (MLA_ROPE // 2)
    cos_t = jnp.tile(cos.reshape(n, -1), (1, reps))
    sin_t = jnp.tile(sin.reshape(n, -1), (1, reps))
    q_h = _mla_q_up(proj, q_norm, w_uq, cos_t, sin_t, b, s, cq_blk=c0 // MLA_Q_RANK)
    k_h, v_h = _mla_kv_up(proj, kv_norm, w_ukv, cos_t, sin_t, b, s,
                          ckv_blk=(c0 + MLA_Q_RANK) // LANES,
                          kr_blk=(c0 + MLA_Q_RANK + MLA_KV_RANK) // LANES)
    o_mla = _mla_attention(q_h, k_h, v_h)
    mix = jnp.concatenate([o_dn, o_mla], axis=-1).reshape(n, -1)
    return _mm(mix, w_out, epi='residual', epi_args=(x, gate), rows_per_batch=s, name='even_out_proj')


def _odd_token_mixer(x, mod, norm_g, w_in, pos_k, pos_v, ck1, ck2, cv1, cv2, w_out, b, s):
    shift, scale, gate = mod
    n = b * s
    g = NSA_GROUPS
    d = NSA_HEAD_DIM
    n_in = NSA_Q_W + 6 * NSA_KV_W + 3 * NSA_HEADS
    w_all = _pad_cols(w_in, -(-n_in // LANES) * LANES)
    proj = _mm(x, w_all, pro='adaln', pro_args=(norm_g, scale, shift), rows_per_batch=s,
               tn_cap=4096, out_dtype=BF16, name='odd_in_proj')
    proj3 = proj.reshape(b, s, -1)

    def kv(i):
        lo = NSA_Q_W + i * NSA_KV_W
        return proj3[:, :, lo:lo + NSA_KV_W].reshape(b, s, g, d)

    k_cmp = _compress(kv(0), pos_k, ck1, ck2)
    v_cmp = _compress(kv(1), pos_v, cv1, cv2)
    to_t = lambda t: t.transpose(0, 2, 3, 1).astype(BF16)
    to_r = lambda t: t.transpose(0, 2, 1, 3).astype(BF16)
    gl = proj3[:, :, NSA_Q_W + 6 * NSA_KV_W:n_in].reshape(b, s, g, NSA_HPG, 3)
    gl_t = gl.transpose(0, 2, 4, 3, 1).reshape(b, g, 3 * NSA_HPG, s)
    gl_t = jnp.pad(gl_t.astype(F32), ((0, 0), (0, 0), (0, 16 - 3 * NSA_HPG), (0, 0)))
    o = _nsa_attention(proj3, gl_t, k_cmp.transpose(0, 1, 3, 2).astype(BF16), v_cmp.astype(BF16),
                       to_t(kv(2)), to_r(kv(3)), to_t(kv(4)), to_r(kv(5)))
    return _mm(o.reshape(n, -1), w_out, epi='residual', epi_args=(x, gate), rows_per_batch=s,
               name='odd_out_proj')


def _moe(x, mod, norm_g, w_router, b_router, w1, w3, w2, layer, b, s):
    shift, scale, gate = mod
    n = b * s
    h, logits = _norm_router(x, norm_g, scale, shift, w_router, s)
    logits = logits[:, :N_EXPERTS] + b_router.astype(F32)
    top_val, top_idx = lax.top_k(logits, TOP_K)
    gate_w = jax.nn.softmax(top_val, axis=-1)
    flat_e = top_idx.reshape(-1)
    onehot = (flat_e[:, None] == jnp.arange(N_EXPERTS, dtype=jnp.int32)[None, :]).astype(jnp.int32)
    rank = jnp.take_along_axis(jnp.cumsum(onehot, axis=0), flat_e[:, None], axis=1)[:, 0] - 1
    counts = jnp.sum(onehot, axis=0)
    padded = ((counts + MOE_BLOCK - 1) // MOE_BLOCK) * MOE_BLOCK
    pad_end = jnp.cumsum(padded)
    pad_start = pad_end - padded
    dest = pad_start[flat_e] + rank
    n_assign = n * TOP_K
    n_blk = -(-n_assign // MOE_BLOCK) + N_EXPERTS
    n_rows = n_blk * MOE_BLOCK
    row_assign = jnp.zeros((n_rows,), jnp.int32).at[dest].set(
        jnp.arange(n_assign, dtype=jnp.int32), unique_indices=True)
    row_tok = row_assign // TOP_K
    blk_exp = jnp.minimum(jnp.searchsorted(pad_end, jnp.arange(n_blk, dtype=jnp.int32) * MOE_BLOCK,
                                           side='right'), N_EXPERTS - 1).astype(jnp.int32)
    n_used = (pad_end[-1:] // MOE_BLOCK).astype(jnp.int32)
    row_w = gate_w.reshape(-1)[row_assign]
    y_rows = _swiglu_grouped(h[row_tok], row_w[:, None], blk_exp, n_used, w1, w3, w2, layer)
    dest_tok = dest.reshape(n, TOP_K)
    return _moe_combine(x, [y_rows[dest_tok[:, j]] for j in range(TOP_K)], gate, s)


def kernel(x, c, positions, ada_w, ada_b, norm_g, final_g, ev_w_in, ev_conv_w, ev_a_log, ev_dt_bias, ev_dn_norm, ev_q_norm, ev_kv_norm, ev_w_uq, ev_w_ukv, ev_w_out, ev_ff_gate, ev_ff_up, ev_ff_down, od_w_in, od_cmp_pos_k, od_cmp_pos_v, od_cmp_k1, od_cmp_k2, od_cmp_v1, od_cmp_v2, od_w_out, od_router, od_router_b, od_moe_w1, od_moe_w3, od_moe_w2):
    b, s, dm = x.shape
    depth = ada_w.shape[0]
    inv = 1.0 / (ROPE_THETA ** (jnp.arange(0, MLA_ROPE, 2, dtype=F32) / MLA_ROPE))
    ang = positions.astype(F32)[..., None] * inv
    cos, sin = jnp.cos(ang), jnp.sin(ang)
    mods = jnp.einsum('bd,lkde->lkbe', jax.nn.silu(c), ada_w,
                      precision=lax.Precision.HIGHEST) + ada_b[:, :, None, :]

    def mod(layer, k):
        m = mods[layer, k]
        return tuple(m[:, None, i * dm:(i + 1) * dm] for i in range(3))

    xf = x.reshape(b * s, dm)
    for layer in range(depth):
        j = layer // 2
        if layer % 2 == 0:
            xf = _even_token_mixer(xf, mod(layer, 0), norm_g[layer, 0], cos, sin, ev_w_in[j],
                                   ev_conv_w[j], ev_a_log[j], ev_dt_bias[j], ev_dn_norm[j],
                                   ev_q_norm[j], ev_kv_norm[j], ev_w_uq[j], ev_w_ukv[j],
                                   ev_w_out[j], b, s)
            shift, scale, gate = mod(layer, 1)
            xf = _swiglu_dense(xf, norm_g[layer, 1], scale, shift, ev_ff_gate[j], ev_ff_up[j],
                               ev_ff_down[j], gate, s)
        else:
            xf = _odd_token_mixer(xf, mod(layer, 0), norm_g[layer, 0], od_w_in[j],
                                  od_cmp_pos_k[j], od_cmp_pos_v[j], od_cmp_k1[j], od_cmp_k2[j],
                                  od_cmp_v1[j], od_cmp_v2[j], od_w_out[j], b, s)
            xf = _moe(xf, mod(layer, 1), norm_g[layer, 1], od_router[j], od_router_b[j],
                      od_moe_w1, od_moe_w3, od_moe_w2, j, b, s)
    return _final_norm(xf, final_g).reshape(b, s, dm)
```

```python
import functools
import math

import jax
import jax.numpy as jnp
import numpy as np
from jax import lax
from jax.experimental import pallas as pl
from jax.experimental.pallas import tpu as pltpu

F32 = jnp.float32
BF16 = jnp.bfloat16

NORM_EPS = 1e-6
NEG_INF = -1e30
TINY = 1e-30

LANES = 128

DN_HEADS = 4
DN_HEAD_DIM = 128
DN_CONV = 4
DN_CHUNK = 64
DN_W = DN_HEADS * DN_HEAD_DIM

MLA_HEADS = 4
MLA_Q_RANK = 256
MLA_KV_RANK = 128
MLA_NOPE = 128
MLA_ROPE = 64
MLA_V = 128
MLA_QK = MLA_NOPE + MLA_ROPE
ROPE_THETA = 10000.0

NSA_HEADS = 16
NSA_GROUPS = 4
NSA_HPG = NSA_HEADS // NSA_GROUPS
NSA_HEAD_DIM = 64
NSA_Q_W = NSA_HEADS * NSA_HEAD_DIM
NSA_KV_W = NSA_GROUPS * NSA_HEAD_DIM
CMP_LEN = 32
CMP_STRIDE = 16
CMP_HIDDEN = 256
SLC_LEN = 64
SLC_TOPN = 16
WIN = 512
FORCE_SCORE = 1e9

N_EXPERTS = 8
TOP_K = 2
MOE_BLOCK = 512

VMEM_LIMIT = 56 * 1024 * 1024


def _params(sem):
    return pltpu.CompilerParams(dimension_semantics=sem, vmem_limit_bytes=VMEM_LIMIT)


def _sigmoid(x):
    return 1.0 / (1.0 + jnp.exp(-x))


def _silu(x):
    return x * _sigmoid(x)


def _dot(a, b):
    return jnp.dot(a.astype(BF16), b.astype(BF16), preferred_element_type=F32)


def _dot_nt(a, b):
    return lax.dot_general(a.astype(BF16), b.astype(BF16), (((1,), (1,)), ((), ())),
                           preferred_element_type=F32)


def _dot_tn(a, b):
    return lax.dot_general(a.astype(BF16), b.astype(BF16), (((0,), (0,)), ((), ())),
                           preferred_element_type=F32)


def _split3(x):
    hi = x.astype(BF16)
    r1 = x - hi.astype(F32)
    mid = r1.astype(BF16)
    lo = (r1 - mid.astype(F32)).astype(BF16)
    return hi, mid, lo


def _dot_lhs01(a01, x):
    hi, mid, lo = _split3(x)
    a = a01.astype(BF16)
    d = functools.partial(jnp.dot, preferred_element_type=F32)
    return d(a, hi) + d(a, mid) + d(a, lo)


def _dot_rhs01(x, b01):
    hi, mid, lo = _split3(x)
    b = b01.astype(BF16)
    d = functools.partial(jnp.dot, preferred_element_type=F32)
    return d(hi, b) + d(mid, b) + d(lo, b)


def _dot_hi(a, b):
    ah = a.astype(BF16)
    al = (a - ah.astype(F32)).astype(BF16)
    bh = b.astype(BF16)
    bl = (b - bh.astype(F32)).astype(BF16)
    d = functools.partial(jnp.dot, preferred_element_type=F32)
    return d(ah, bh) + d(ah, bl) + d(al, bh)


def _rms(x, gain):
    return x * lax.rsqrt(jnp.mean(x * x, axis=-1, keepdims=True) + NORM_EPS) * gain


def _mm_body(*refs, pro, epi):
    x_ref, w_ref = refs[0], refs[1]
    pos = 2
    if pro == 'adaln':
        g_ref, sc_ref, sh_ref = refs[pos:pos + 3]
        pos += 3
    elif pro == 'rms':
        g_ref = refs[pos]
        pos += 1
    if epi == 'residual':
        res_ref, gate_ref = refs[pos:pos + 2]
        pos += 2
    o_ref = refs[pos]
    pos += 1
    if pro is not None:
        h_ref = refs[pos]

        @pl.when(pl.program_id(1) == 0)
        def _():
            y = _rms(x_ref[...].astype(F32), g_ref[...])
            if pro == 'adaln':
                y = y * (1.0 + sc_ref[...]) + sh_ref[...]
            h_ref[...] = y.astype(BF16)

        h = h_ref[...]
    else:
        h = x_ref[...].astype(BF16)
    acc = jnp.dot(h, w_ref[...], preferred_element_type=F32)
    if epi == 'residual':
        acc = res_ref[...] + gate_ref[...] * acc
    o_ref[...] = acc.astype(o_ref.dtype)


def _pick_tile(n, cap):
    best = LANES
    t = LANES
    while t <= min(n, cap):
        if n % t == 0:
            best = t
        t += LANES
    return best


def _mm(x, w, *, pro=None, pro_args=(), epi=None, epi_args=(), rows_per_batch=None,
        out_dtype=F32, tm=512, tn_cap=1024, name='mm'):
    m, k = x.shape
    n = w.shape[1]
    tm = min(tm, m)
    tn = _pick_tile(n, tn_cap)
    assert m % tm == 0 and n % tn == 0
    rpb = None if rows_per_batch is None else rows_per_batch // tm
    in_specs = [pl.BlockSpec((tm, k), lambda i, j: (i, 0)),
                pl.BlockSpec((k, tn), lambda i, j: (0, j))]
    args = [x, w.astype(BF16)]
    scratch = []
    if pro == 'adaln':
        g, sc, sh = pro_args
        in_specs += [pl.BlockSpec((1, k), lambda i, j: (0, 0)),
                     pl.BlockSpec((None, 1, k), lambda i, j: (i // rpb, 0, 0)),
                     pl.BlockSpec((None, 1, k), lambda i, j: (i // rpb, 0, 0))]
        args += [g.reshape(1, k), sc, sh]
    elif pro == 'rms':
        in_specs += [pl.BlockSpec((1, k), lambda i, j: (0, 0))]
        args += [pro_args[0].reshape(1, k)]
    if pro is not None:
        scratch = [pltpu.VMEM((tm, k), BF16)]
    if epi == 'residual':
        res, gate = epi_args
        in_specs += [pl.BlockSpec((tm, tn), lambda i, j: (i, j)),
                     pl.BlockSpec((None, 1, tn), lambda i, j: (i // rpb, 0, j))]
        args += [res, gate]
    return pl.pallas_call(
        functools.partial(_mm_body, pro=pro, epi=epi),
        out_shape=jax.ShapeDtypeStruct((m, n), out_dtype),
        grid=(m // tm, n // tn),
        in_specs=in_specs,
        out_specs=pl.BlockSpec((tm, tn), lambda i, j: (i, j)),
        scratch_shapes=scratch,
        compiler_params=_params(("parallel", "arbitrary")),
        name=name,
    )(*args)


def _swiglu_dense_body(x_ref, g_ref, sc_ref, sh_ref, wg_ref, wu_ref, wd_ref, gate_ref, o_ref,
                       h_ref, acc_ref):
    j = pl.program_id(1)

    @pl.when(j == 0)
    def _():
        y = _rms(x_ref[...], g_ref[...]) * (1.0 + sc_ref[...]) + sh_ref[...]
        h_ref[...] = y.astype(BF16)
        acc_ref[...] = jnp.zeros_like(acc_ref)

    h = h_ref[...]
    a = jnp.dot(h, wg_ref[...], preferred_element_type=F32)
    b = jnp.dot(h, wu_ref[...], preferred_element_type=F32)
    act = (_silu(a) * b).astype(BF16)
    acc_ref[...] += jnp.dot(act, wd_ref[...], preferred_element_type=F32)

    @pl.when(j == pl.num_programs(1) - 1)
    def _():
        o_ref[...] = x_ref[...] + gate_ref[...] * acc_ref[...]


def _swiglu_dense(x, g, sc, sh, wg, wu, wd, gate, rows_per_batch, *, tm=1024, tf_cap=256):
    m, k = x.shape
    f = wg.shape[1]
    tf = _pick_tile(f, tf_cap)
    assert m % tm == 0 and f % tf == 0
    rpb = rows_per_batch // tm
    return pl.pallas_call(
        _swiglu_dense_body,
        out_shape=jax.ShapeDtypeStruct((m, k), F32),
        grid=(m // tm, f // tf),
        in_specs=[pl.BlockSpec((tm, k), lambda i, j: (i, 0)),
                  pl.BlockSpec((1, k), lambda i, j: (0, 0)),
                  pl.BlockSpec((None, 1, k), lambda i, j: (i // rpb, 0, 0)),
                  pl.BlockSpec((None, 1, k), lambda i, j: (i // rpb, 0, 0)),
                  pl.BlockSpec((k, tf), lambda i, j: (0, j)),
                  pl.BlockSpec((k, tf), lambda i, j: (0, j)),
                  pl.BlockSpec((tf, k), lambda i, j: (j, 0)),
                  pl.BlockSpec((None, 1, k), lambda i, j: (i // rpb, 0, 0))],
        out_specs=pl.BlockSpec((tm, k), lambda i, j: (i, 0)),
        scratch_shapes=[pltpu.VMEM((tm, k), BF16), pltpu.VMEM((tm, k), F32)],
        compiler_params=_params(("parallel", "arbitrary")),
        name='swiglu_dense',
    )(x, g.reshape(1, k), sc, sh, wg.astype(BF16), wu.astype(BF16), wd.astype(BF16), gate)


def _swiglu_grouped_body(be_ref, nu_ref, x_ref, rw_ref, wg_ref, wu_ref, wd_ref, o_ref, acc_ref):
    i = pl.program_id(0)
    j = pl.program_id(1)
    used = i < nu_ref[0]

    @pl.when(j == 0)
    def _():
        acc_ref[...] = jnp.zeros_like(acc_ref)

    @pl.when(used)
    def _():
        h = x_ref[...]
        a = jnp.dot(h, wg_ref[...], preferred_element_type=F32)
        b = jnp.dot(h, wu_ref[...], preferred_element_type=F32)
        act = (_silu(a) * b).astype(BF16)
        acc_ref[...] += jnp.dot(act, wd_ref[...], preferred_element_type=F32)

    @pl.when(j == pl.num_programs(1) - 1)
    def _():
        o_ref[...] = (acc_ref[...] * rw_ref[...]).astype(o_ref.dtype)


def _swiglu_grouped(x_rows, row_w, blk_exp, n_used, w1, w3, w2, layer, *, tm=MOE_BLOCK,
                    tf_cap=1792):
    n_rows, k = x_rows.shape
    f = w1.shape[3]
    tf = _pick_tile(f, tf_cap)
    assert n_rows % tm == 0 and f % tf == 0
    nj = f // tf

    def jj(i, j, nu):
        return jnp.where(i < nu[0], j, nj - 1)

    grid_spec = pltpu.PrefetchScalarGridSpec(
        num_scalar_prefetch=2,
        grid=(n_rows // tm, nj),
        in_specs=[pl.BlockSpec((tm, k), lambda i, j, be, nu: (i, 0)),
                  pl.BlockSpec((tm, 1), lambda i, j, be, nu: (i, 0)),
                  pl.BlockSpec((None, None, k, tf),
                               lambda i, j, be, nu: (layer, be[i], 0, jj(i, j, nu))),
                  pl.BlockSpec((None, None, k, tf),
                               lambda i, j, be, nu: (layer, be[i], 0, jj(i, j, nu))),
                  pl.BlockSpec((None, None, tf, k),
                               lambda i, j, be, nu: (layer, be[i], jj(i, j, nu), 0))],
        out_specs=pl.BlockSpec((tm, k), lambda i, j, be, nu: (i, 0)),
        scratch_shapes=[pltpu.VMEM((tm, k), F32)],
    )
    return pl.pallas_call(
        _swiglu_grouped_body,
        out_shape=jax.ShapeDtypeStruct((n_rows, k), BF16),
        grid_spec=grid_spec,
        compiler_params=_params(("arbitrary", "arbitrary")),
        name='swiglu_grouped',
    )(blk_exp, n_used, x_rows, row_w, w1.astype(BF16), w3.astype(BF16), w2.astype(BF16))


def _moe_combine_body(x_ref, gate_ref, *refs):
    y_refs, o_ref = refs[:-1], refs[-1]
    total = y_refs[0][...].astype(F32)
    for y_ref in y_refs[1:]:
        total = total + y_ref[...].astype(F32)
    o_ref[...] = x_ref[...] + gate_ref[...] * total


def _moe_combine(x, y_slots, gate, rows_per_batch, *, tm=512):
    m, k = x.shape
    rpb = rows_per_batch // tm
    row_blk = pl.BlockSpec((tm, k), lambda i: (i, 0))
    return pl.pallas_call(
        _moe_combine_body,
        out_shape=jax.ShapeDtypeStruct((m, k), F32),
        grid=(m // tm,),
        in_specs=[row_blk, pl.BlockSpec((None, 1, k), lambda i: (i // rpb, 0, 0))]
                 + [row_blk] * len(y_slots),
        out_specs=row_blk,
        compiler_params=_params(("parallel",)),
        name='moe_combine',
    )(x, gate, *y_slots)


def _norm_router_body(x_ref, g_ref, sc_ref, sh_ref, wr_ref, h_ref, lg_ref):
    y = _rms(x_ref[...], g_ref[...]) * (1.0 + sc_ref[...]) + sh_ref[...]
    h_ref[...] = y.astype(BF16)
    lg_ref[...] = jnp.dot(y, wr_ref[...], preferred_element_type=F32,
                          precision=lax.Precision.HIGHEST)


def _norm_router(x, g, sc, sh, w_router, rows_per_batch, *, tm=512):
    m, k = x.shape
    e = w_router.shape[1]
    wr = jnp.zeros((k, LANES), F32).at[:, :e].set(w_router)
    rpb = rows_per_batch // tm
    return pl.pallas_call(
        _norm_router_body,
        out_shape=(jax.ShapeDtypeStruct((m, k), BF16), jax.ShapeDtypeStruct((m, LANES), F32)),
        grid=(m // tm,),
        in_specs=[pl.BlockSpec((tm, k), lambda i: (i, 0)),
                  pl.BlockSpec((1, k), lambda i: (0, 0)),
                  pl.BlockSpec((None, 1, k), lambda i: (i // rpb, 0, 0)),
                  pl.BlockSpec((None, 1, k), lambda i: (i // rpb, 0, 0)),
                  pl.BlockSpec((k, LANES), lambda i: (0, 0))],
        out_specs=(pl.BlockSpec((tm, k), lambda i: (i, 0)),
                   pl.BlockSpec((tm, LANES), lambda i: (i, 0))),
        compiler_params=_params(("parallel",)),
        name='norm_router',
    )(x, g.reshape(1, k), sc, sh, wr)


def _final_norm_body(x_ref, g_ref, o_ref):
    o_ref[...] = _rms(x_ref[...], g_ref[...])


def _final_norm(x, g, *, tm=1024):
    m, k = x.shape
    return pl.pallas_call(
        _final_norm_body,
        out_shape=jax.ShapeDtypeStruct((m, k), F32),
        grid=(m // tm,),
        in_specs=[pl.BlockSpec((tm, k), lambda i: (i, 0)), pl.BlockSpec((1, k), lambda i: (0, 0))],
        out_specs=pl.BlockSpec((tm, k), lambda i: (i, 0)),
        compiler_params=_params(("parallel",)),
        name='final_norm',
    )(x, g.reshape(1, k))


def _softplus(x):
    return jnp.maximum(x, 0.0) + jnp.log(1.0 + jnp.exp(-jnp.abs(x)))


CONV_HALO = 8
DN_HEADS_PER_STEP = 4


def _deltanet_body(q_ref, k_ref, v_ref, z_ref, pb_ref, pa_ref, cwq_ref, cwk_ref, cwv_ref, alog_ref,
                   dtb_ref, gn_ref, o_ref, s_ref, xs_ref, *, n_chunks):
    c_len = DN_CHUNK
    tile = n_chunks * c_len

    @pl.when(pl.program_id(1) == 0)
    def _():
        s_ref[...] = jnp.zeros_like(s_ref)
        xs_ref[:, 0:CONV_HALO, :] = jnp.zeros((3, CONV_HALO, xs_ref.shape[-1]), F32)

    def conv(i, x_ref, cw_ref):
        xs_ref[i, CONV_HALO:CONV_HALO + tile, :] = x_ref[...]
        y = None
        for tap in range(DN_CONV):
            off = CONV_HALO - (DN_CONV - 1) + tap
            term = xs_ref[i, off:off + tile, :] * cw_ref[tap:tap + 1, :]
            y = term if y is None else y + term
        xs_ref[i, 0:CONV_HALO, :] = xs_ref[i, tile:tile + CONV_HALO, :]
        return _silu(y)

    q = conv(0, q_ref, cwq_ref)
    k = conv(1, k_ref, cwk_ref)
    v = conv(2, v_ref, cwv_ref)

    row = lax.broadcasted_iota(jnp.int32, (c_len, c_len), 0)
    col = lax.broadcasted_iota(jnp.int32, (c_len, c_len), 1)
    incl = row >= col
    strict = row > col
    incl_f = incl.astype(F32)
    strict_f = strict.astype(F32)
    eye = (row == col).astype(F32)
    dk = DN_HEAD_DIM
    hp = q.shape[-1] // dk
    head = lambda x, h: x[:, h * dk:(h + 1) * dk]

    def l2n(x, scale):
        return jnp.concatenate(
            [head(x, h) * (lax.rsqrt(jnp.sum(head(x, h) * head(x, h), axis=-1, keepdims=True)
                                     + NORM_EPS) * scale) for h in range(hp)], axis=1)

    q = l2n(q, dk ** -0.5)
    k = l2n(k, 1.0)
    beta = _sigmoid(pb_ref[...])
    g = -jnp.exp(alog_ref[...]) * _softplus(pa_ref[...] + dtb_ref[...])
    kb = k * beta
    vb = v * beta
    chunks = range(hp * n_chunks)

    def cs(x, i):
        h, c = divmod(i, n_chunks)
        return x[c * c_len:(c + 1) * c_len, h * dk:(h + 1) * dk]
    gg = [_dot_lhs01(incl_f, jnp.concatenate([cs(g, c), cs(g, c)[:, :c_len] * strict_f], axis=1))
          for c in chunks]
    gc = [x[:, :dk] for x in gg]
    decay = [jnp.where(incl, jnp.exp(jnp.where(incl, x[:, dk:dk + c_len], 0.0)), 0.0) for x in gg]
    eg = [jnp.exp(x) for x in gc]
    kq = [_dot_nt(jnp.concatenate([cs(kb, c), cs(q, c)], axis=0), cs(k, c)) for c in chunks]
    lower = [jnp.where(strict, kq[c][:c_len] * decay[c], 0.0) for c in chunks]
    qk = [kq[c][c_len:] * decay[c] for c in chunks]
    tinv = [eye - lo for lo in lower]
    pw = lower
    for _ in range(5):
        pw = [_dot(p, p) for p in pw]
        tinv = [t + _dot(t, p) for t, p in zip(tinv, pw)]
    wu = [_dot(tinv[c], jnp.concatenate([cs(kb, c) * eg[c], cs(vb, c)], axis=1)) for c in chunks]
    g_last = [x[c_len - 1:c_len, :] for x in gc]
    k_tail = [cs(k, c) * jnp.exp(g_last[c] - gc[c]) for c in chunks]
    mb = [_dot_tn(k_tail[c], wu[c]) for c in chunks]
    qo = [_dot(qk[c], wu[c]) for c in chunks]
    lhs = [jnp.concatenate([cs(q, c) * eg[c] - qo[c][:, :dk], mb[c][:, :dk]], axis=0)
           for c in chunks]
    z = z_ref[...]
    for h in range(hp):
        state = s_ref[h]
        outs = []
        for c in range(h * n_chunks, (h + 1) * n_chunks):
            prod = _dot(lhs[c], state)
            outs.append(prod[:c_len] + qo[c][:, dk:])
            state = state * jnp.exp(g_last[c]) - prod[c_len:] + mb[c][:, dk:]
        s_ref[h] = state
        o = jnp.concatenate(outs, axis=0)
        o_ref[:, h * dk:(h + 1) * dk] = (_rms(o, gn_ref[...]) * _silu(head(z, h))).astype(o_ref.dtype)


def _deltanet(proj, conv_w, a_log, dt_bias, dn_norm, *, z_blk, pb_blk, pa_blk, tile=512):
    b, s, _ = proj.shape
    h = DN_HEADS
    d = DN_HEAD_DIM
    tile = min(tile, s)
    hp = DN_HEADS_PER_STEP
    ng = h // hp
    assert s % tile == 0 and tile % DN_CHUNK == 0 and h % hp == 0
    assert all(off % hp == 0 for off in (h, z_blk, pb_blk, pa_blk))
    rep = lambda t: jnp.repeat(t.astype(F32), d).reshape(1, h * d)

    def col(off):
        return pl.BlockSpec((None, tile, hp * d),
                            lambda i, t, off=off: (i // ng, t, off // hp + i % ng))

    def cw(off):
        return pl.BlockSpec((DN_CONV, hp * d), lambda i, t, off=off: (0, off // hp + i % ng))

    per_head = pl.BlockSpec((1, hp * d), lambda i, t: (0, i % ng))
    return pl.pallas_call(
        functools.partial(_deltanet_body, n_chunks=tile // DN_CHUNK),
        out_shape=jax.ShapeDtypeStruct((b, s, h * d), BF16),
        grid=(b * ng, s // tile),
        in_specs=[col(0), col(h), col(2 * h), col(z_blk), col(pb_blk), col(pa_blk),
                  cw(0), cw(h), cw(2 * h),
                  per_head, per_head, pl.BlockSpec((1, d), lambda i, t: (0, 0))],
        out_specs=col(0),
        scratch_shapes=[pltpu.VMEM((hp, d, d), F32),
                        pltpu.VMEM((3, tile + CONV_HALO, hp * d), F32)],
        compiler_params=_params(("parallel", "arbitrary")),
        name='deltanet',
    )(proj, proj, proj, proj, proj, proj, conv_w, conv_w, conv_w, rep(a_log), rep(dt_bias),
      dn_norm.reshape(1, d))


LOG2E = 1.4426950408889634


def _rope_halves(x1, x2, cos, sin):
    return x1 * cos - x2 * sin, x2 * cos + x1 * sin


def _mla_q_up_body(cq_ref, g_ref, w_ref, cos_ref, sin_ref, o_ref):
    hm, half = MLA_HEADS, MLA_ROPE // 2
    x = _rms(cq_ref[...], g_ref[...]).astype(BF16)
    acc = jnp.dot(x, w_ref[...], preferred_element_type=F32)
    nope_w = hm * MLA_NOPE
    r1, r2 = _rope_halves(acc[:, nope_w:nope_w + hm * half], acc[:, nope_w + hm * half:],
                          cos_ref[...], sin_ref[...])
    for h in range(hm):
        q_h = jnp.concatenate([acc[:, h * MLA_NOPE:(h + 1) * MLA_NOPE],
                               r1[:, h * half:(h + 1) * half], r2[:, h * half:(h + 1) * half]], axis=1)
        o_ref[h] = (q_h * (MLA_QK ** -0.5 * LOG2E)).astype(o_ref.dtype)


def _mla_q_up(proj, q_norm, w_uq, cos_t, sin_t, b, s, *, cq_blk, tm=512):
    n = proj.shape[0]
    hm, half = MLA_HEADS, MLA_ROPE // 2
    assert hm * half == LANES and s % tm == 0
    heads = np.arange(hm)[:, None] * MLA_QK
    cols = np.concatenate([(heads + np.arange(MLA_NOPE)[None, :]).ravel(),
                           (heads + MLA_NOPE + np.arange(half)[None, :]).ravel(),
                           (heads + MLA_NOPE + half + np.arange(half)[None, :]).ravel()])
    rpb = s // tm
    return pl.pallas_call(
        _mla_q_up_body,
        out_shape=jax.ShapeDtypeStruct((b, hm, s, MLA_QK), BF16),
        grid=(n // tm,),
        in_specs=[pl.BlockSpec((tm, MLA_Q_RANK), lambda i: (i, cq_blk)),
                  pl.BlockSpec((1, MLA_Q_RANK), lambda i: (0, 0)),
                  pl.BlockSpec((MLA_Q_RANK, hm * MLA_QK), lambda i: (0, 0)),
                  pl.BlockSpec((tm, LANES), lambda i: (i, 0)),
                  pl.BlockSpec((tm, LANES), lambda i: (i, 0))],
        out_specs=pl.BlockSpec((None, hm, tm, MLA_QK), lambda i: (i // rpb, 0, i % rpb, 0)),
        compiler_params=_params(("parallel",)),
        name='mla_q_up',
    )(proj, q_norm.reshape(1, -1), w_uq[:, cols].astype(BF16), cos_t, sin_t)


def _mla_kv_up_body(ckv_ref, kr_ref, g_ref, w_ref, cos_ref, sin_ref, k_ref, v_ref):
    hm, half = MLA_HEADS, MLA_ROPE // 2
    x = _rms(ckv_ref[...], g_ref[...]).astype(BF16)
    acc = jnp.dot(x, w_ref[...], preferred_element_type=F32)
    kr = kr_ref[...]
    r1, r2 = _rope_halves(kr[:, :half], kr[:, half:2 * half], cos_ref[:, :half], sin_ref[:, :half])
    for h in range(hm):
        k_h = jnp.concatenate([acc[:, h * MLA_NOPE:(h + 1) * MLA_NOPE], r1, r2], axis=1)
        k_ref[h] = k_h.astype(k_ref.dtype)
        v_lo = hm * MLA_NOPE + h * MLA_V
        v_ref[h] = acc[:, v_lo:v_lo + MLA_V].astype(v_ref.dtype)


def _mla_kv_up(proj, kv_norm, w_ukv, cos_t, sin_t, b, s, *, ckv_blk, kr_blk, tm=512):
    n = proj.shape[0]
    hm = MLA_HEADS
    assert MLA_KV_RANK == LANES and MLA_V == LANES and s % tm == 0
    heads = np.arange(hm)[:, None] * (MLA_NOPE + MLA_V)
    cols = np.concatenate([(heads + np.arange(MLA_NOPE)[None, :]).ravel(),
                           (heads + MLA_NOPE + np.arange(MLA_V)[None, :]).ravel()])
    rpb = s // tm
    head_major = lambda w: pl.BlockSpec((None, hm, tm, w), lambda i: (i // rpb, 0, i % rpb, 0))
    return pl.pallas_call(
        _mla_kv_up_body,
        out_shape=(jax.ShapeDtypeStruct((b, hm, s, MLA_QK), BF16),
                   jax.ShapeDtypeStruct((b, hm, s, MLA_V), BF16)),
        grid=(n // tm,),
        in_specs=[pl.BlockSpec((tm, LANES), lambda i: (i, ckv_blk)),
                  pl.BlockSpec((tm, LANES), lambda i: (i, kr_blk)),
                  pl.BlockSpec((1, LANES), lambda i: (0, 0)),
                  pl.BlockSpec((MLA_KV_RANK, hm * (MLA_NOPE + MLA_V)), lambda i: (0, 0)),
                  pl.BlockSpec((tm, LANES), lambda i: (i, 0)),
                  pl.BlockSpec((tm, LANES), lambda i: (i, 0))],
        out_specs=(head_major(MLA_QK), head_major(MLA_V)),
        compiler_params=_params(("parallel",)),
        name='mla_kv_up',
    )(proj, proj, kv_norm.reshape(1, -1), w_ukv[:, cols].astype(BF16), cos_t, sin_t)


MLA_HEADS_PER_STEP = 2


def _mla_body(q_ref, k_ref, v_ref, o_ref, *, tq, tk):
    qi = pl.program_id(2)
    hp = q_ref.shape[0]
    dv = v_ref.shape[-1]
    qs = [q_ref[h] for h in range(hp)]

    def step(kt, carry, masked):
        ks = pl.multiple_of(kt * tk, tk)
        out = []
        for h in range(hp):
            m, l, acc = carry[h]
            s = lax.dot_general(qs[h], k_ref[h, pl.ds(ks, tk), :], (((1,), (1,)), ((), ())),
                                preferred_element_type=F32)
            if masked:
                tpos = lax.broadcasted_iota(jnp.int32, (tq, tk), 0)
                kpos = lax.broadcasted_iota(jnp.int32, (tq, tk), 1)
                s = jnp.where(kpos <= tpos, s, NEG_INF)
            m_new = jnp.maximum(m, jnp.max(s, axis=-1, keepdims=True))
            alpha = jnp.exp2(m - m_new)
            p = jnp.exp2(s - m_new)
            l = alpha * l + jnp.sum(p, axis=-1, keepdims=True)
            acc = alpha * acc + jnp.dot(p.astype(BF16), v_ref[h, pl.ds(ks, tk), :],
                                        preferred_element_type=F32)
            out.append((m_new, l, acc))
        return tuple(out)

    init = tuple((jnp.full((tq, 1), NEG_INF, F32), jnp.zeros((tq, 1), F32),
                  jnp.zeros((tq, dv), F32)) for _ in range(hp))
    carry = lax.fori_loop(0, qi, lambda kt, c: step(kt, c, False), init)
    carry = step(qi, carry, True)
    for h in range(hp):
        _, l, acc = carry[h]
        o_ref[:, h * dv:(h + 1) * dv] = (acc / l).astype(o_ref.dtype)


def _mla_attention(q, k, v, *, tq=512, tk=512):
    b, h, s, dqk = q.shape
    dv = v.shape[-1]
    hp = MLA_HEADS_PER_STEP
    tq = min(tq, s)
    tk = min(tk, tq)
    assert s % tq == 0 and tq == tk and h % hp == 0
    return pl.pallas_call(
        functools.partial(_mla_body, tq=tq, tk=tk),
        out_shape=jax.ShapeDtypeStruct((b, s, h * dv), BF16),
        grid=(b, h // hp, s // tq),
        in_specs=[pl.BlockSpec((None, hp, tq, dqk), lambda bi, hi, qi: (bi, hi, qi, 0)),
                  pl.BlockSpec((None, hp, s, dqk), lambda bi, hi, qi: (bi, hi, 0, 0)),
                  pl.BlockSpec((None, hp, s, dv), lambda bi, hi, qi: (bi, hi, 0, 0))],
        out_specs=pl.BlockSpec((None, tq, hp * dv), lambda bi, hi, qi: (bi, qi, hi)),
        compiler_params=_params(("parallel", "parallel", "arbitrary")),
        name='mla_attention',
    )(q, k, v)


SUBLANES = 8


def _compress_body(x_ref, pos_ref, w1_ref, w2_ref, o_ref, sh_ref):
    n, half = x_ref.shape
    x = x_ref[...].astype(F32)
    first = jnp.dot((x + pos_ref[0:1, :]).astype(BF16), w1_ref[0:half, :],
                    preferred_element_type=F32)
    second = jnp.dot((x + pos_ref[1:2, :]).astype(BF16), w1_ref[half:2 * half, :],
                     preferred_element_type=F32)
    sh_ref[0:n, :] = second
    sh_ref[n:n + SUBLANES, :] = jnp.zeros((SUBLANES, sh_ref.shape[-1]), F32)
    hid = _silu(first + sh_ref[1:n + 1, :])
    o_ref[...] = jnp.dot(hid.astype(BF16), w2_ref[...], preferred_element_type=F32)


def _compress(t, pos_emb, w1, w2):
    assert CMP_LEN == 2 * CMP_STRIDE
    b, s, g, d = t.shape
    n_chunk = s // CMP_STRIDE
    half = CMP_STRIDE * d
    ch = t.reshape(b, n_chunk, CMP_STRIDE, g, d).transpose(0, 3, 1, 2, 4).reshape(b, g, n_chunk, half)
    w2p = jnp.zeros((CMP_HIDDEN, LANES), BF16).at[:, :d].set(w2.astype(BF16))
    out = pl.pallas_call(
        _compress_body,
        out_shape=jax.ShapeDtypeStruct((b, g, n_chunk, LANES), F32),
        grid=(b, g),
        in_specs=[pl.BlockSpec((None, None, n_chunk, half), lambda bi, gi: (bi, gi, 0, 0)),
                  pl.BlockSpec((2, half), lambda bi, gi: (0, 0)),
                  pl.BlockSpec((2 * half, CMP_HIDDEN), lambda bi, gi: (0, 0)),
                  pl.BlockSpec((CMP_HIDDEN, LANES), lambda bi, gi: (0, 0))],
        out_specs=pl.BlockSpec((None, None, n_chunk, LANES), lambda bi, gi: (bi, gi, 0, 0)),
        scratch_shapes=[pltpu.VMEM((n_chunk + SUBLANES, CMP_HIDDEN), F32)],
        compiler_params=_params(("parallel", "parallel")),
        name='nsa_compress',
    )(ch, pos_emb.reshape(2, half), w1.astype(BF16), w2p)
    return out[..., :d]


MASK_BIG = 1e30
NSA_GROUPS_PER_STEP = 1


def _nsa_body(q_ref, gl_ref, kct_ref, vc_ref, kse_ref, vs_ref, kwt_ref, vw_ref, o_ref, *,
              tq, tk, n_blocks, n_top, span):
    hg = NSA_HPG
    d = NSA_HEAD_DIM
    rows = hg * tq
    qi = pl.program_id(2)
    q0 = qi * tq
    gp = kct_ref.shape[0]
    qw = hg * d
    t_q = q0 + lax.broadcasted_iota(jnp.int32, (tq, 1), 0)

    def add_bias(s, bias):
        n = s.shape[-1]
        return (s.reshape(hg, tq, n) + bias[None]).reshape(rows, n)

    n_cmp = kct_ref.shape[-1]
    cmp_end = lax.broadcasted_iota(jnp.int32, (1, n_cmp), 1) * CMP_STRIDE + (CMP_LEN - 1)
    bias_c = jnp.where(cmp_end <= t_q, 0.0, NEG_INF)
    has_c = jnp.concatenate([t_q >= CMP_LEN - 1] * hg, axis=0)
    jj = lax.broadcasted_iota(jnp.int32, (n_cmp, LANES), 0)
    nn = lax.broadcasted_iota(jnp.int32, (n_cmp, LANES), 1)
    per = SLC_LEN // CMP_STRIDE
    pool = 0.5 * ((jj // per == nn).astype(F32) + ((jj + 1) // per == nn).astype(F32))
    blk = lax.broadcasted_iota(jnp.int32, (LANES, LANES), 0)
    blk_f = blk.astype(F32)

    def before_loop(g):
        qf = q_ref[:, g * qw:(g + 1) * qw].astype(F32) * (d ** -0.5 * LOG2E)
        q = jnp.concatenate([qf[:, h * d:(h + 1) * d] for h in range(hg)], axis=0).astype(BF16)
        s_c = add_bias(jnp.dot(q, kct_ref[g], preferred_element_type=F32), bias_c)
        e_c = jnp.exp2(s_c - jnp.max(s_c, axis=-1, keepdims=True))
        acc_c = jnp.dot(e_c.astype(BF16), vc_ref[g], preferred_element_type=F32)
        inv_c = jnp.where(has_c, 1.0 / jnp.maximum(acc_c[:, d:d + 1], TINY), 0.0)
        p_sum = jnp.sum((e_c * inv_c).reshape(hg, tq, n_cmp), axis=0)
        imp = _dot_rhs01(p_sum, pool)
        sel_parts = []
        for r in range(tq // LANES):
            imp_t = imp[r * LANES:(r + 1) * LANES].T
            cur = (q0 + r * LANES + lax.broadcasted_iota(jnp.int32, (1, LANES), 1)) // SLC_LEN
            forced = (blk == 0) | (blk == cur) | (blk == cur - 1)
            causal = blk <= cur
            imp_t = jnp.where(causal & jnp.logical_not(forced), imp_t, -2.0)
            sel_t = jnp.zeros((LANES, LANES), F32)
            for _ in range(n_top - 3):
                top = jnp.max(imp_t, axis=0, keepdims=True)
                first = jnp.min(jnp.where(imp_t == top, blk_f, float(LANES)), axis=0, keepdims=True)
                hit = blk_f == first
                sel_t = jnp.where(hit, 1.0, sel_t)
                imp_t = jnp.where(hit, -2.0, imp_t)
            sel_t = jnp.where(cur >= n_top, jnp.where(forced, 1.0, sel_t),
                              jnp.where(causal, 1.0, 0.0))
            sel_parts.append(sel_t.T)
        sel = jnp.concatenate(sel_parts, axis=0)
        unsel = jnp.concatenate([sel - 1.0] * hg, axis=0)
        q_aug = jnp.concatenate([unsel.astype(BF16), q], axis=1)
        return q, q_aug, acc_c, inv_c

    pre = [before_loop(g) for g in range(gp)]
    t_rel = t_q - lax.broadcasted_iota(jnp.int32, (1, tk), 1)

    def sel_tile(kt, carry, causal):
        ks = pl.multiple_of(kt * tk, tk)
        bias = jnp.where(t_rel >= ks, 0.0, NEG_INF) if causal else None
        out = []
        for g in range(gp):
            m, acc = carry[g]
            s = jnp.dot(pre[g][1], kse_ref[g, :, pl.ds(ks, tk)], preferred_element_type=F32)
            if causal:
                s = add_bias(s, bias)
            m_new = jnp.maximum(m, jnp.max(s, axis=-1, keepdims=True))
            p = jnp.exp2(s - m_new)
            acc = jnp.exp2(m - m_new) * acc + jnp.dot(p.astype(BF16), vs_ref[g, pl.ds(ks, tk), :],
                                                      preferred_element_type=F32)
            out.append((m_new, acc))
        return tuple(out)

    init = tuple((jnp.full((rows, 1), NEG_INF, F32), jnp.zeros((rows, vs_ref.shape[-1]), F32))
                 for _ in range(gp))
    n_kt = (q0 + tq + tk - 1) // tk
    carry = lax.fori_loop(0, n_kt - 1, lambda kt, c: sel_tile(kt, c, False), init)
    carry = sel_tile(n_kt - 1, carry, True)

    ws = pl.multiple_of(jnp.maximum(q0 + tq - span, 0), LANES)
    dist = (t_q - ws) - lax.broadcasted_iota(jnp.int32, (1, span), 1)
    bias_w = jnp.where((dist >= 0) & (dist < WIN), 0.0, NEG_INF)

    def after_loop(g):
        q, _, acc_c, inv_c = pre[g]
        acc_s = carry[g][1]
        inv_s = 1.0 / jnp.maximum(acc_s[:, d:d + 1], TINY)
        s_w = add_bias(jnp.dot(q, kwt_ref[g, :, pl.ds(ws, span)], preferred_element_type=F32),
                       bias_w)
        e_w = jnp.exp2(s_w - jnp.max(s_w, axis=-1, keepdims=True))
        acc_w = jnp.dot(e_w.astype(BF16), vw_ref[g, pl.ds(ws, span), :],
                        preferred_element_type=F32)
        inv_w = 1.0 / acc_w[:, d:d + 1]
        gates = _sigmoid(gl_ref[g])
        g_t = []
        for r in range(tq // LANES):
            blk_g = jnp.concatenate([gates[:, r * LANES:(r + 1) * LANES],
                                     jnp.zeros((LANES - gates.shape[0], LANES), F32)], axis=0)
            g_t.append(blk_g.T)
        g_t = jnp.concatenate(g_t, axis=0)

        def gate_col(branch):
            return jnp.concatenate([g_t[:, branch * hg + h:branch * hg + h + 1] for h in range(hg)],
                                   axis=0)

        o = ((gate_col(0) * inv_c) * acc_c[:, :d] + (gate_col(1) * inv_s) * acc_s[:, :d]
             + (gate_col(2) * inv_w) * acc_w[:, :d])
        o_ref[:, g * qw:(g + 1) * qw] = jnp.concatenate(
            [o[h * tq:(h + 1) * tq] for h in range(hg)], axis=1).astype(o_ref.dtype)

    for g in range(gp):
        after_loop(g)


def _nsa_attention(proj, gl_t, kct, vc, kst, vs, kwt, vw, *, tq=512, tk=512):
    b, s, _ = proj.shape
    g = NSA_GROUPS
    d = NSA_HEAD_DIM
    n_cmp = kct.shape[-1]
    tq = min(tq, s)
    tk = min(tk, s)
    span = min(WIN + tq, s)
    n_blocks = s // SLC_LEN
    gp = NSA_GROUPS_PER_STEP
    assert s % tq == 0 and s % tk == 0 and tq % LANES == 0 and n_blocks <= LANES and g % gp == 0
    kv_t = lambda n, r=d: pl.BlockSpec((None, gp, r, n), lambda bi, gi, qi: (bi, gi, 0, 0))
    kv_r = lambda n: pl.BlockSpec((None, gp, n, LANES), lambda bi, gi, qi: (bi, gi, 0, 0))

    def with_ones(v):
        tail = jnp.zeros(v.shape[:-1] + (LANES - d,), BF16).at[..., 0].set(1.0)
        return jnp.concatenate([v, tail], axis=-1)

    blk_of_key = jnp.arange(s, dtype=jnp.int32) // SLC_LEN
    mask_rows = jnp.where(jnp.arange(LANES, dtype=jnp.int32)[:, None] == blk_of_key[None, :],
                          MASK_BIG, 0.0).astype(BF16)
    kse = jnp.concatenate([jnp.broadcast_to(mask_rows, (b, g, LANES, s)), kst], axis=2)
    vc, vs, vw = with_ones(vc), with_ones(vs), with_ones(vw)
    kst = kse
    return pl.pallas_call(
        functools.partial(_nsa_body, tq=tq, tk=tk, n_blocks=n_blocks,
                          n_top=min(SLC_TOPN, n_blocks), span=span),
        out_shape=jax.ShapeDtypeStruct((b, s, g * NSA_HPG * d), BF16),
        grid=(b, g // gp, s // tq),
        in_specs=[pl.BlockSpec((None, tq, gp * NSA_HPG * d), lambda bi, gi, qi: (bi, qi, gi)),
                  pl.BlockSpec((None, gp, 16, tq), lambda bi, gi, qi: (bi, gi, 0, qi)),
                  kv_t(n_cmp), kv_r(n_cmp), kv_t(s, LANES + d), kv_r(s), kv_t(s), kv_r(s)],
        out_specs=pl.BlockSpec((None, tq, gp * NSA_HPG * d), lambda bi, gi, qi: (bi, qi, gi)),
        compiler_params=_params(("parallel", "parallel", "arbitrary")),
        name='nsa_attention',
    )(proj, gl_t, kct, vc, kst, vs, kwt, vw)


def _pad_cols(w, n):
    return jnp.pad(w, ((0, 0), (0, n - w.shape[1])))


def _even_token_mixer(x, mod, norm_g, cos, sin, w_in, conv_w, a_log, dt_bias, dn_norm, q_norm,
                      kv_norm, w_uq, w_ukv, w_out, b, s):
    shift, scale, gate = mod
    n = b * s
    hd = DN_HEADS
    o_b = 4 * DN_W
    o_a = o_b + hd
    o_cq = o_a + hd
    o_ckv = o_cq + MLA_Q_RANK
    o_kr = o_ckv + MLA_KV_RANK
    w_cols = [w_in[:, :o_b], w_in[:, o_cq:o_ckv], w_in[:, o_ckv:o_kr],
              _pad_cols(w_in[:, o_kr:o_kr + MLA_ROPE], LANES),
              jnp.repeat(w_in[:, o_b:o_a], DN_HEAD_DIM, axis=1),
              jnp.repeat(w_in[:, o_a:o_cq], DN_HEAD_DIM, axis=1)]
    w_all = jnp.concatenate(w_cols, axis=1)
    proj = _mm(x, w_all, pro='adaln', pro_args=(norm_g, scale, shift), rows_per_batch=s,
               tn_cap=4096, name='even_in_proj')
    width = proj.shape[1]
    c0 = 4 * DN_W
    pb_blk = (c0 + MLA_Q_RANK + MLA_KV_RANK + LANES) // LANES
    proj3 = proj.reshape(b, s, width)

    o_dn = _deltanet(proj3, conv_w.astype(F32), a_log, dt_bias, dn_norm, z_blk=3 * hd,
                     pb_blk=pb_blk, pa_blk=pb_blk + hd)

    reps = LANES // (MLA_ROPE // 2)
    cos_t = jnp.tile(cos.reshape(n, -1), (1, reps))
    sin_t = jnp.tile(sin.reshape(n, -1), (1, reps))
    q_h = _mla_q_up(proj, q_norm, w_uq, cos_t, sin_t, b, s, cq_blk=c0 // MLA_Q_RANK)
    k_h, v_h = _mla_kv_up(proj, kv_norm, w_ukv, cos_t, sin_t, b, s,
                          ckv_blk=(c0 + MLA_Q_RANK) // LANES,
                          kr_blk=(c0 + MLA_Q_RANK + MLA_KV_RANK) // LANES)
    o_mla = _mla_attention(q_h, k_h, v_h)
    mix = jnp.concatenate([o_dn, o_mla], axis=-1).reshape(n, -1)
    return _mm(mix, w_out, epi='residual', epi_args=(x, gate), rows_per_batch=s, name='even_out_proj')


def _odd_token_mixer(x, mod, norm_g, w_in, pos_k, pos_v, ck1, ck2, cv1, cv2, w_out, b, s):
    shift, scale, gate = mod
    n = b * s
    g = NSA_GROUPS
    d = NSA_HEAD_DIM
    n_in = NSA_Q_W + 6 * NSA_KV_W + 3 * NSA_HEADS
    w_all = _pad_cols(w_in, -(-n_in // LANES) * LANES)
    proj = _mm(x, w_all, pro='adaln', pro_args=(norm_g, scale, shift), rows_per_batch=s,
               tn_cap=4096, out_dtype=BF16, name='odd_in_proj')
    proj3 = proj.reshape(b, s, -1)

    def kv(i):
        lo = NSA_Q_W + i * NSA_KV_W
        return proj3[:, :, lo:lo + NSA_KV_W].reshape(b, s, g, d)

    k_cmp = _compress(kv(0), pos_k, ck1, ck2)
    v_cmp = _compress(kv(1), pos_v, cv1, cv2)
    to_t = lambda t: t.transpose(0, 2, 3, 1).astype(BF16)
    to_r = lambda t: t.transpose(0, 2, 1, 3).astype(BF16)
    gl = proj3[:, :, NSA_Q_W + 6 * NSA_KV_W:n_in].reshape(b, s, g, NSA_HPG, 3)
    gl_t = gl.transpose(0, 2, 4, 3, 1).reshape(b, g, 3 * NSA_HPG, s)
    gl_t = jnp.pad(gl_t.astype(F32), ((0, 0), (0, 0), (0, 16 - 3 * NSA_HPG), (0, 0)))
    o = _nsa_attention(proj3, gl_t, k_cmp.transpose(0, 1, 3, 2).astype(BF16), v_cmp.astype(BF16),
                       to_t(kv(2)), to_r(kv(3)), to_t(kv(4)), to_r(kv(5)))
    return _mm(o.reshape(n, -1), w_out, epi='residual', epi_args=(x, gate), rows_per_batch=s,
               name='odd_out_proj')


def _moe(x, mod, norm_g, w_router, b_router, w1, w3, w2, layer, b, s):
    shift, scale, gate = mod
    n = b * s
    h, logits = _norm_router(x, norm_g, scale, shift, w_router, s)
    logits = logits[:, :N_EXPERTS] + b_router.astype(F32)
    top_val, top_idx = lax.top_k(logits, TOP_K)
    gate_w = jax.nn.softmax(top_val, axis=-1)
    flat_e = top_idx.reshape(-1)
    onehot = (flat_e[:, None] == jnp.arange(N_EXPERTS, dtype=jnp.int32)[None, :]).astype(jnp.int32)
    rank = jnp.take_along_axis(jnp.cumsum(onehot, axis=0), flat_e[:, None], axis=1)[:, 0] - 1
    counts = jnp.sum(onehot, axis=0)
    padded = ((counts + MOE_BLOCK - 1) // MOE_BLOCK) * MOE_BLOCK
    pad_end = jnp.cumsum(padded)
    pad_start = pad_end - padded
    dest = pad_start[flat_e] + rank
    n_assign = n * TOP_K
    n_blk = -(-n_assign // MOE_BLOCK) + N_EXPERTS
    n_rows = n_blk * MOE_BLOCK
    row_assign = jnp.zeros((n_rows,), jnp.int32).at[dest].set(
        jnp.arange(n_assign, dtype=jnp.int32), unique_indices=True)
    row_tok = row_assign // TOP_K
    blk_exp = jnp.minimum(jnp.searchsorted(pad_end, jnp.arange(n_blk, dtype=jnp.int32) * MOE_BLOCK,
                                           side='right'), N_EXPERTS - 1).astype(jnp.int32)
    n_used = (pad_end[-1:] // MOE_BLOCK).astype(jnp.int32)
    row_w = gate_w.reshape(-1)[row_assign]
    y_rows = _swiglu_grouped(h[row_tok], row_w[:, None], blk_exp, n_used, w1, w3, w2, layer)
    dest_tok = dest.reshape(n, TOP_K)
    return _moe_combine(x, [y_rows[dest_tok[:, j]] for j in range(TOP_K)], gate, s)


def kernel(x, c, positions, ada_w, ada_b, norm_g, final_g, ev_w_in, ev_conv_w, ev_a_log, ev_dt_bias, ev_dn_norm, ev_q_norm, ev_kv_norm, ev_w_uq, ev_w_ukv, ev_w_out, ev_ff_gate, ev_ff_up, ev_ff_down, od_w_in, od_cmp_pos_k, od_cmp_pos_v, od_cmp_k1, od_cmp_k2, od_cmp_v1, od_cmp_v2, od_w_out, od_router, od_router_b, od_moe_w1, od_moe_w3, od_moe_w2):
    b, s, dm = x.shape
    depth = ada_w.shape[0]
    inv = 1.0 / (ROPE_THETA ** (jnp.arange(0, MLA_ROPE, 2, dtype=F32) / MLA_ROPE))
    ang = positions.astype(F32)[..., None] * inv
    cos, sin = jnp.cos(ang), jnp.sin(ang)
    mods = jnp.einsum('bd,lkde->lkbe', jax.nn.silu(c), ada_w,
                      precision=lax.Precision.HIGHEST) + ada_b[:, :, None, :]

    def mod(layer, k):
        m = mods[layer, k]
        return tuple(m[:, None, i * dm:(i + 1) * dm] for i in range(3))

    xf = x.reshape(b * s, dm)
    for layer in range(depth):
        j = layer // 2
        if layer % 2 == 0:
            xf = _even_token_mixer(xf, mod(layer, 0), norm_g[layer, 0], cos, sin, ev_w_in[j],
                                   ev_conv_w[j], ev_a_log[j], ev_dt_bias[j], ev_dn_norm[j],
                                   ev_q_norm[j], ev_kv_norm[j], ev_w_uq[j], ev_w_ukv[j],
                                   ev_w_out[j], b, s)
            shift, scale, gate = mod(layer, 1)
            xf = _swiglu_dense(xf, norm_g[layer, 1], scale, shift, ev_ff_gate[j], ev_ff_up[j],
                               ev_ff_down[j], gate, s)
        else:
            xf = _odd_token_mixer(xf, mod(layer, 0), norm_g[layer, 0], od_w_in[j],
                                  od_cmp_pos_k[j], od_cmp_pos_v[j], od_cmp_k1[j], od_cmp_k2[j],
                                  od_cmp_v1[j], od_cmp_v2[j], od_w_out[j], b, s)
            xf = _moe(xf, mod(layer, 1), norm_g[layer, 1], od_router[j], od_router_b[j],
                      od_moe_w1, od_moe_w3, od_moe_w2, j, b, s)
    return _final_norm(xf, final_g).reshape(b, s, dm)
```

```python
import functools

import jax
import jax.numpy as jnp
import numpy as np
from jax import lax
from jax.experimental import pallas as pl
from jax.experimental.pallas import tpu as pltpu

F32 = jnp.float32
BF16 = jnp.bfloat16

NORM_EPS = 1e-6
NEG_INF = -1e30
TINY = 1e-30

LANES = 128

DN_HEADS = 4
DN_HEAD_DIM = 128
DN_CONV = 4
DN_CHUNK = 64
DN_W = DN_HEADS * DN_HEAD_DIM

MLA_HEADS = 4
MLA_Q_RANK = 256
MLA_KV_RANK = 128
MLA_NOPE = 128
MLA_ROPE = 64
MLA_V = 128
MLA_QK = MLA_NOPE + MLA_ROPE
ROPE_THETA = 10000.0

NSA_HEADS = 16
NSA_GROUPS = 4
NSA_HPG = NSA_HEADS // NSA_GROUPS
NSA_HEAD_DIM = 64
NSA_Q_W = NSA_HEADS * NSA_HEAD_DIM
NSA_KV_W = NSA_GROUPS * NSA_HEAD_DIM
CMP_LEN = 32
CMP_STRIDE = 16
CMP_HIDDEN = 256
SLC_LEN = 64
SLC_TOPN = 16
WIN = 512
N_FORCED = 3

N_EXPERTS = 8
TOP_K = 2
MOE_BLOCK = 512

VMEM_LIMIT = 56 * 1024 * 1024


def _params(sem):
    return pltpu.CompilerParams(dimension_semantics=sem, vmem_limit_bytes=VMEM_LIMIT)


def _sigmoid(x):
    return 1.0 / (1.0 + jnp.exp(-x))


def _silu(x):
    return x * _sigmoid(x)


def _dot(a, b):
    return jnp.dot(a.astype(BF16), b.astype(BF16), preferred_element_type=F32)


def _dot_nt(a, b):
    return lax.dot_general(a.astype(BF16), b.astype(BF16), (((1,), (1,)), ((), ())),
                           preferred_element_type=F32)


def _dot_tn(a, b):
    return lax.dot_general(a.astype(BF16), b.astype(BF16), (((0,), (0,)), ((), ())),
                           preferred_element_type=F32)


def _split3(x):
    hi = x.astype(BF16)
    r1 = x - hi.astype(F32)
    mid = r1.astype(BF16)
    lo = (r1 - mid.astype(F32)).astype(BF16)
    return hi, mid, lo


def _dot_lhs01(a01, x):
    hi, mid, lo = _split3(x)
    a = a01.astype(BF16)
    d = functools.partial(jnp.dot, preferred_element_type=F32)
    return d(a, hi) + d(a, mid) + d(a, lo)


def _rms(x, gain):
    return x * lax.rsqrt(jnp.mean(x * x, axis=-1, keepdims=True) + NORM_EPS) * gain


def _mm_body(*refs, pro, epi):
    x_ref, w_ref = refs[0], refs[1]
    pos = 2
    if pro == 'adaln':
        g_ref, sc_ref, sh_ref = refs[pos:pos + 3]
        pos += 3
    elif pro == 'rms':
        g_ref = refs[pos]
        pos += 1
    if epi == 'residual':
        res_ref, gate_ref = refs[pos:pos + 2]
        pos += 2
    o_ref = refs[pos]
    pos += 1
    if pro is not None:
        h_ref = refs[pos]

        @pl.when(pl.program_id(1) == 0)
        def _():
            y = _rms(x_ref[...].astype(F32), g_ref[...])
            if pro == 'adaln':
                y = y * (1.0 + sc_ref[...]) + sh_ref[...]
            h_ref[...] = y.astype(BF16)

        h = h_ref[...]
    else:
        h = x_ref[...].astype(BF16)
    acc = jnp.dot(h, w_ref[...], preferred_element_type=F32)
    if epi == 'residual':
        acc = res_ref[...] + gate_ref[...] * acc
    o_ref[...] = acc.astype(o_ref.dtype)


def _pick_tile(n, cap):
    best = LANES
    t = LANES
    while t <= min(n, cap):
        if n % t == 0:
            best = t
        t += LANES
    return best


def _mm(x, w, *, pro=None, pro_args=(), epi=None, epi_args=(), rows_per_batch=None,
        out_dtype=F32, tm=512, tn_cap=1024, name='mm'):
    m, k = x.shape
    n = w.shape[1]
    tm = min(tm, m)
    tn = _pick_tile(n, tn_cap)
    assert m % tm == 0 and n % tn == 0
    rpb = None if rows_per_batch is None else rows_per_batch // tm
    in_specs = [pl.BlockSpec((tm, k), lambda i, j: (i, 0)),
                pl.BlockSpec((k, tn), lambda i, j: (0, j))]
    args = [x, w.astype(BF16)]
    scratch = []
    if pro == 'adaln':
        g, sc, sh = pro_args
        in_specs += [pl.BlockSpec((1, k), lambda i, j: (0, 0)),
                     pl.BlockSpec((None, 1, k), lambda i, j: (i // rpb, 0, 0)),
                     pl.BlockSpec((None, 1, k), lambda i, j: (i // rpb, 0, 0))]
        args += [g.reshape(1, k), sc, sh]
    elif pro == 'rms':
        in_specs += [pl.BlockSpec((1, k), lambda i, j: (0, 0))]
        args += [pro_args[0].reshape(1, k)]
    if pro is not None:
        scratch = [pltpu.VMEM((tm, k), BF16)]
    if epi == 'residual':
        res, gate = epi_args
        in_specs += [pl.BlockSpec((tm, tn), lambda i, j: (i, j)),
                     pl.BlockSpec((None, 1, tn), lambda i, j: (i // rpb, 0, j))]
        args += [res, gate]
    return pl.pallas_call(
        functools.partial(_mm_body, pro=pro, epi=epi),
        out_shape=jax.ShapeDtypeStruct((m, n), out_dtype),
        grid=(m // tm, n // tn),
        in_specs=in_specs,
        out_specs=pl.BlockSpec((tm, tn), lambda i, j: (i, j)),
        scratch_shapes=scratch,
        compiler_params=_params(("parallel", "arbitrary")),
        name=name,
    )(*args)


def _swiglu_dense_body(x_ref, g_ref, sc_ref, sh_ref, wg_ref, wu_ref, wd_ref, gate_ref, o_ref,
                       h_ref, acc_ref):
    j = pl.program_id(1)

    @pl.when(j == 0)
    def _():
        y = _rms(x_ref[...], g_ref[...]) * (1.0 + sc_ref[...]) + sh_ref[...]
        h_ref[...] = y.astype(BF16)
        acc_ref[...] = jnp.zeros_like(acc_ref)

    h = h_ref[...]
    a = jnp.dot(h, wg_ref[...], preferred_element_type=F32)
    b = jnp.dot(h, wu_ref[...], preferred_element_type=F32)
    act = (_silu(a) * b).astype(BF16)
    acc_ref[...] += jnp.dot(act, wd_ref[...], preferred_element_type=F32)

    @pl.when(j == pl.num_programs(1) - 1)
    def _():
        o_ref[...] = x_ref[...] + gate_ref[...] * acc_ref[...]


def _swiglu_dense(x, g, sc, sh, wg, wu, wd, gate, rows_per_batch, *, tm=1024, tf_cap=256):
    m, k = x.shape
    f = wg.shape[1]
    tf = _pick_tile(f, tf_cap)
    assert m % tm == 0 and f % tf == 0
    rpb = rows_per_batch // tm
    return pl.pallas_call(
        _swiglu_dense_body,
        out_shape=jax.ShapeDtypeStruct((m, k), F32),
        grid=(m // tm, f // tf),
        in_specs=[pl.BlockSpec((tm, k), lambda i, j: (i, 0)),
                  pl.BlockSpec((1, k), lambda i, j: (0, 0)),
                  pl.BlockSpec((None, 1, k), lambda i, j: (i // rpb, 0, 0)),
                  pl.BlockSpec((None, 1, k), lambda i, j: (i // rpb, 0, 0)),
                  pl.BlockSpec((k, tf), lambda i, j: (0, j)),
                  pl.BlockSpec((k, tf), lambda i, j: (0, j)),
                  pl.BlockSpec((tf, k), lambda i, j: (j, 0)),
                  pl.BlockSpec((None, 1, k), lambda i, j: (i // rpb, 0, 0))],
        out_specs=pl.BlockSpec((tm, k), lambda i, j: (i, 0)),
        scratch_shapes=[pltpu.VMEM((tm, k), BF16), pltpu.VMEM((tm, k), F32)],
        compiler_params=_params(("parallel", "arbitrary")),
        name='swiglu_dense',
    )(x, g.reshape(1, k), sc, sh, wg.astype(BF16), wu.astype(BF16), wd.astype(BF16), gate)


def _swiglu_grouped_body(be_ref, nu_ref, x_ref, rw_ref, wg_ref, wu_ref, wd_ref, o_ref, acc_ref):
    i = pl.program_id(0)
    j = pl.program_id(1)
    used = i < nu_ref[0]

    @pl.when(j == 0)
    def _():
        acc_ref[...] = jnp.zeros_like(acc_ref)

    @pl.when(used)
    def _():
        h = x_ref[...]
        a = jnp.dot(h, wg_ref[...], preferred_element_type=F32)
        b = jnp.dot(h, wu_ref[...], preferred_element_type=F32)
        act = (_silu(a) * b).astype(BF16)
        acc_ref[...] += jnp.dot(act, wd_ref[...], preferred_element_type=F32)

    @pl.when(j == pl.num_programs(1) - 1)
    def _():
        o_ref[...] = (acc_ref[...] * rw_ref[...]).astype(o_ref.dtype)


def _swiglu_grouped(x_rows, row_w, blk_exp, n_used, w1, w3, w2, layer, *, tm=MOE_BLOCK,
                    tf_cap=1792):
    n_rows, k = x_rows.shape
    f = w1.shape[3]
    tf = _pick_tile(f, tf_cap)
    assert n_rows % tm == 0 and f % tf == 0
    nj = f // tf

    def jj(i, j, nu):
        return jnp.where(i < nu[0], j, nj - 1)

    grid_spec = pltpu.PrefetchScalarGridSpec(
        num_scalar_prefetch=2,
        grid=(n_rows // tm, nj),
        in_specs=[pl.BlockSpec((tm, k), lambda i, j, be, nu: (i, 0)),
                  pl.BlockSpec((tm, 1), lambda i, j, be, nu: (i, 0)),
                  pl.BlockSpec((None, None, k, tf),
                               lambda i, j, be, nu: (layer, be[i], 0, jj(i, j, nu))),
                  pl.BlockSpec((None, None, k, tf),
                               lambda i, j, be, nu: (layer, be[i], 0, jj(i, j, nu))),
                  pl.BlockSpec((None, None, tf, k),
                               lambda i, j, be, nu: (layer, be[i], jj(i, j, nu), 0))],
        out_specs=pl.BlockSpec((tm, k), lambda i, j, be, nu: (i, 0)),
        scratch_shapes=[pltpu.VMEM((tm, k), F32)],
    )
    return pl.pallas_call(
        _swiglu_grouped_body,
        out_shape=jax.ShapeDtypeStruct((n_rows, k), BF16),
        grid_spec=grid_spec,
        compiler_params=_params(("arbitrary", "arbitrary")),
        name='swiglu_grouped',
    )(blk_exp, n_used, x_rows, row_w, w1.astype(BF16), w3.astype(BF16), w2.astype(BF16))


def _moe_combine_body(x_ref, gate_ref, *refs):
    y_refs, o_ref = refs[:-1], refs[-1]
    total = y_refs[0][...].astype(F32)
    for y_ref in y_refs[1:]:
        total = total + y_ref[...].astype(F32)
    o_ref[...] = x_ref[...] + gate_ref[...] * total


def _moe_combine(x, y_slots, gate, rows_per_batch, *, tm=512):
    m, k = x.shape
    rpb = rows_per_batch // tm
    row_blk = pl.BlockSpec((tm, k), lambda i: (i, 0))
    return pl.pallas_call(
        _moe_combine_body,
        out_shape=jax.ShapeDtypeStruct((m, k), F32),
        grid=(m // tm,),
        in_specs=[row_blk, pl.BlockSpec((None, 1, k), lambda i: (i // rpb, 0, 0))]
                 + [row_blk] * len(y_slots),
        out_specs=row_blk,
        compiler_params=_params(("parallel",)),
        name='moe_combine',
    )(x, gate, *y_slots)


def _norm_router_body(x_ref, g_ref, sc_ref, sh_ref, wr_ref, h_ref, lg_ref):
    y = _rms(x_ref[...], g_ref[...]) * (1.0 + sc_ref[...]) + sh_ref[...]
    h_ref[...] = y.astype(BF16)
    lg_ref[...] = jnp.dot(y, wr_ref[...], preferred_element_type=F32,
                          precision=lax.Precision.HIGHEST)


def _norm_router(x, g, sc, sh, w_router, rows_per_batch, *, tm=512):
    m, k = x.shape
    e = w_router.shape[1]
    wr = jnp.zeros((k, LANES), F32).at[:, :e].set(w_router)
    rpb = rows_per_batch // tm
    return pl.pallas_call(
        _norm_router_body,
        out_shape=(jax.ShapeDtypeStruct((m, k), BF16), jax.ShapeDtypeStruct((m, LANES), F32)),
        grid=(m // tm,),
        in_specs=[pl.BlockSpec((tm, k), lambda i: (i, 0)),
                  pl.BlockSpec((1, k), lambda i: (0, 0)),
                  pl.BlockSpec((None, 1, k), lambda i: (i // rpb, 0, 0)),
                  pl.BlockSpec((None, 1, k), lambda i: (i // rpb, 0, 0)),
                  pl.BlockSpec((k, LANES), lambda i: (0, 0))],
        out_specs=(pl.BlockSpec((tm, k), lambda i: (i, 0)),
                   pl.BlockSpec((tm, LANES), lambda i: (i, 0))),
        compiler_params=_params(("parallel",)),
        name='norm_router',
    )(x, g.reshape(1, k), sc, sh, wr)


def _final_norm_body(x_ref, g_ref, o_ref):
    o_ref[...] = _rms(x_ref[...], g_ref[...])


def _final_norm(x, g, *, tm=1024):
    m, k = x.shape
    return pl.pallas_call(
        _final_norm_body,
        out_shape=jax.ShapeDtypeStruct((m, k), F32),
        grid=(m // tm,),
        in_specs=[pl.BlockSpec((tm, k), lambda i: (i, 0)), pl.BlockSpec((1, k), lambda i: (0, 0))],
        out_specs=pl.BlockSpec((tm, k), lambda i: (i, 0)),
        compiler_params=_params(("parallel",)),
        name='final_norm',
    )(x, g.reshape(1, k))


def _softplus(x):
    return jnp.maximum(x, 0.0) + jnp.log(1.0 + jnp.exp(-jnp.abs(x)))


CONV_HALO = 8
DN_HEADS_PER_STEP = 4


def _deltanet_body(q_ref, k_ref, v_ref, z_ref, pb_ref, pa_ref, cwq_ref, cwk_ref, cwv_ref, alog_ref,
                   dtb_ref, gn_ref, o_ref, s_ref, xs_ref, *, n_chunks):
    c_len = DN_CHUNK
    tile = n_chunks * c_len

    @pl.when(pl.program_id(1) == 0)
    def _():
        s_ref[...] = jnp.zeros_like(s_ref)
        xs_ref[:, 0:CONV_HALO, :] = jnp.zeros((3, CONV_HALO, xs_ref.shape[-1]), F32)

    def conv(i, x_ref, cw_ref):
        xs_ref[i, CONV_HALO:CONV_HALO + tile, :] = x_ref[...]
        y = None
        for tap in range(DN_CONV):
            off = CONV_HALO - (DN_CONV - 1) + tap
            term = xs_ref[i, off:off + tile, :] * cw_ref[tap:tap + 1, :]
            y = term if y is None else y + term
        xs_ref[i, 0:CONV_HALO, :] = xs_ref[i, tile:tile + CONV_HALO, :]
        return _silu(y)

    q = conv(0, q_ref, cwq_ref)
    k = conv(1, k_ref, cwk_ref)
    v = conv(2, v_ref, cwv_ref)

    row = lax.broadcasted_iota(jnp.int32, (c_len, c_len), 0)
    col = lax.broadcasted_iota(jnp.int32, (c_len, c_len), 1)
    incl = row >= col
    strict = row > col
    incl_f = incl.astype(F32)
    strict_f = strict.astype(F32)
    eye = (row == col).astype(F32)
    dk = DN_HEAD_DIM
    hp = q.shape[-1] // dk
    head = lambda x, h: x[:, h * dk:(h + 1) * dk]

    def l2n(x, scale):
        return jnp.concatenate(
            [head(x, h) * (lax.rsqrt(jnp.sum(head(x, h) * head(x, h), axis=-1, keepdims=True)
                                     + NORM_EPS) * scale) for h in range(hp)], axis=1)

    q = l2n(q, dk ** -0.5)
    k = l2n(k, 1.0)
    beta = _sigmoid(pb_ref[...])
    g = -jnp.exp(alog_ref[...]) * _softplus(pa_ref[...] + dtb_ref[...])
    kb = k * beta
    vb = v * beta
    chunks = range(hp * n_chunks)

    def cs(x, i):
        h, c = divmod(i, n_chunks)
        return x[c * c_len:(c + 1) * c_len, h * dk:(h + 1) * dk]
    gg = [_dot_lhs01(incl_f, jnp.concatenate([cs(g, c), cs(g, c)[:, :c_len] * strict_f], axis=1))
          for c in chunks]
    gc = [x[:, :dk] for x in gg]
    decay = [jnp.where(incl, jnp.exp(jnp.where(incl, x[:, dk:dk + c_len], 0.0)), 0.0) for x in gg]
    eg = [jnp.exp(x) for x in gc]
    kq = [_dot_nt(jnp.concatenate([cs(kb, c), cs(q, c)], axis=0), cs(k, c)) for c in chunks]
    lower = [jnp.where(strict, kq[c][:c_len] * decay[c], 0.0) for c in chunks]
    qk = [kq[c][c_len:] * decay[c] for c in chunks]
    tinv = [eye - lo for lo in lower]
    pw = lower
    for _ in range(5):
        pw = [_dot(p, p) for p in pw]
        tinv = [t + _dot(t, p) for t, p in zip(tinv, pw)]
    wu = [_dot(tinv[c], jnp.concatenate([cs(kb, c) * eg[c], cs(vb, c)], axis=1)) for c in chunks]
    g_last = [x[c_len - 1:c_len, :] for x in gc]
    k_tail = [cs(k, c) * jnp.exp(g_last[c] - gc[c]) for c in chunks]
    mb = [_dot_tn(k_tail[c], wu[c]) for c in chunks]
    qo = [_dot(qk[c], wu[c]) for c in chunks]
    lhs = [jnp.concatenate([cs(q, c) * eg[c] - qo[c][:, :dk], mb[c][:, :dk]], axis=0)
           for c in chunks]
    z = z_ref[...]
    for h in range(hp):
        state = s_ref[h]
        outs = []
        for c in range(h * n_chunks, (h + 1) * n_chunks):
            prod = _dot(lhs[c], state)
            outs.append(prod[:c_len] + qo[c][:, dk:])
            state = state * jnp.exp(g_last[c]) - prod[c_len:] + mb[c][:, dk:]
        s_ref[h] = state
        o = jnp.concatenate(outs, axis=0)
        o_ref[:, h * dk:(h + 1) * dk] = (_rms(o, gn_ref[...]) * _silu(head(z, h))).astype(o_ref.dtype)


def _deltanet(proj, conv_w, a_log, dt_bias, dn_norm, *, z_blk, pb_blk, pa_blk, tile=512):
    b, s, _ = proj.shape
    h = DN_HEADS
    d = DN_HEAD_DIM
    tile = min(tile, s)
    hp = DN_HEADS_PER_STEP
    ng = h // hp
    assert s % tile == 0 and tile % DN_CHUNK == 0 and h % hp == 0
    assert all(off % hp == 0 for off in (h, z_blk, pb_blk, pa_blk))
    rep = lambda t: jnp.repeat(t.astype(F32), d).reshape(1, h * d)

    def col(off):
        return pl.BlockSpec((None, tile, hp * d),
                            lambda i, t, off=off: (i // ng, t, off // hp + i % ng))

    def cw(off):
        return pl.BlockSpec((DN_CONV, hp * d), lambda i, t, off=off: (0, off // hp + i % ng))

    per_head = pl.BlockSpec((1, hp * d), lambda i, t: (0, i % ng))
    return pl.pallas_call(
        functools.partial(_deltanet_body, n_chunks=tile // DN_CHUNK),
        out_shape=jax.ShapeDtypeStruct((b, s, h * d), BF16),
        grid=(b * ng, s // tile),
        in_specs=[col(0), col(h), col(2 * h), col(z_blk), col(pb_blk), col(pa_blk),
                  cw(0), cw(h), cw(2 * h),
                  per_head, per_head, pl.BlockSpec((1, d), lambda i, t: (0, 0))],
        out_specs=col(0),
        scratch_shapes=[pltpu.VMEM((hp, d, d), F32),
                        pltpu.VMEM((3, tile + CONV_HALO, hp * d), F32)],
        compiler_params=_params(("parallel", "arbitrary")),
        name='deltanet',
    )(proj, proj, proj, proj, proj, proj, conv_w, conv_w, conv_w, rep(a_log), rep(dt_bias),
      dn_norm.reshape(1, d))


LOG2E = 1.4426950408889634


def _rope_halves(x1, x2, cos, sin):
    return x1 * cos - x2 * sin, x2 * cos + x1 * sin


def _mla_q_up_body(cq_ref, g_ref, w_ref, cos_ref, sin_ref, o_ref):
    hm, half = MLA_HEADS, MLA_ROPE // 2
    x = _rms(cq_ref[...], g_ref[...]).astype(BF16)
    acc = jnp.dot(x, w_ref[...], preferred_element_type=F32)
    nope_w = hm * MLA_NOPE
    r1, r2 = _rope_halves(acc[:, nope_w:nope_w + hm * half], acc[:, nope_w + hm * half:],
                          cos_ref[...], sin_ref[...])
    for h in range(hm):
        q_h = jnp.concatenate([acc[:, h * MLA_NOPE:(h + 1) * MLA_NOPE],
                               r1[:, h * half:(h + 1) * half], r2[:, h * half:(h + 1) * half]], axis=1)
        o_ref[h] = (q_h * (MLA_QK ** -0.5 * LOG2E)).astype(o_ref.dtype)


def _mla_q_up(proj, q_norm, w_uq, cos_t, sin_t, b, s, *, cq_blk, tm=512):
    n = proj.shape[0]
    hm, half = MLA_HEADS, MLA_ROPE // 2
    assert hm * half == LANES and s % tm == 0
    heads = np.arange(hm)[:, None] * MLA_QK
    cols = np.concatenate([(heads + np.arange(MLA_NOPE)[None, :]).ravel(),
                           (heads + MLA_NOPE + np.arange(half)[None, :]).ravel(),
                           (heads + MLA_NOPE + half + np.arange(half)[None, :]).ravel()])
    rpb = s // tm
    return pl.pallas_call(
        _mla_q_up_body,
        out_shape=jax.ShapeDtypeStruct((b, hm, s, MLA_QK), BF16),
        grid=(n // tm,),
        in_specs=[pl.BlockSpec((tm, MLA_Q_RANK), lambda i: (i, cq_blk)),
                  pl.BlockSpec((1, MLA_Q_RANK), lambda i: (0, 0)),
                  pl.BlockSpec((MLA_Q_RANK, hm * MLA_QK), lambda i: (0, 0)),
                  pl.BlockSpec((tm, LANES), lambda i: (i, 0)),
                  pl.BlockSpec((tm, LANES), lambda i: (i, 0))],
        out_specs=pl.BlockSpec((None, hm, tm, MLA_QK), lambda i: (i // rpb, 0, i % rpb, 0)),
        compiler_params=_params(("parallel",)),
        name='mla_q_up',
    )(proj, q_norm.reshape(1, -1), w_uq[:, cols].astype(BF16), cos_t, sin_t)


def _mla_kv_up_body(ckv_ref, kr_ref, g_ref, w_ref, cos_ref, sin_ref, k_ref, v_ref):
    hm, half = MLA_HEADS, MLA_ROPE // 2
    x = _rms(ckv_ref[...], g_ref[...]).astype(BF16)
    acc = jnp.dot(x, w_ref[...], preferred_element_type=F32)
    kr = kr_ref[...]
    r1, r2 = _rope_halves(kr[:, :half], kr[:, half:2 * half], cos_ref[:, :half], sin_ref[:, :half])
    for h in range(hm):
        k_h = jnp.concatenate([acc[:, h * MLA_NOPE:(h + 1) * MLA_NOPE], r1, r2], axis=1)
        k_ref[h] = k_h.astype(k_ref.dtype)
        v_lo = hm * MLA_NOPE + h * MLA_V
        v_ref[h] = acc[:, v_lo:v_lo + MLA_V].astype(v_ref.dtype)


def _mla_kv_up(proj, kv_norm, w_ukv, cos_t, sin_t, b, s, *, ckv_blk, kr_blk, tm=512):
    n = proj.shape[0]
    hm = MLA_HEADS
    assert MLA_KV_RANK == LANES and MLA_V == LANES and s % tm == 0
    heads = np.arange(hm)[:, None] * (MLA_NOPE + MLA_V)
    cols = np.concatenate([(heads + np.arange(MLA_NOPE)[None, :]).ravel(),
                           (heads + MLA_NOPE + np.arange(MLA_V)[None, :]).ravel()])
    rpb = s // tm
    head_major = lambda w: pl.BlockSpec((None, hm, tm, w), lambda i: (i // rpb, 0, i % rpb, 0))
    return pl.pallas_call(
        _mla_kv_up_body,
        out_shape=(jax.ShapeDtypeStruct((b, hm, s, MLA_QK), BF16),
                   jax.ShapeDtypeStruct((b, hm, s, MLA_V), BF16)),
        grid=(n // tm,),
        in_specs=[pl.BlockSpec((tm, LANES), lambda i: (i, ckv_blk)),
                  pl.BlockSpec((tm, LANES), lambda i: (i, kr_blk)),
                  pl.BlockSpec((1, LANES), lambda i: (0, 0)),
                  pl.BlockSpec((MLA_KV_RANK, hm * (MLA_NOPE + MLA_V)), lambda i: (0, 0)),
                  pl.BlockSpec((tm, LANES), lambda i: (i, 0)),
                  pl.BlockSpec((tm, LANES), lambda i: (i, 0))],
        out_specs=(head_major(MLA_QK), head_major(MLA_V)),
        compiler_params=_params(("parallel",)),
        name='mla_kv_up',
    )(proj, proj, kv_norm.reshape(1, -1), w_ukv[:, cols].astype(BF16), cos_t, sin_t)


MLA_HEADS_PER_STEP = 2


def _mla_body(q_ref, k_ref, v_ref, o_ref, *, tq, tk):
    qi = pl.program_id(2)
    hp = q_ref.shape[0]
    dv = v_ref.shape[-1]
    qs = [q_ref[h] for h in range(hp)]

    def step(kt, carry, masked):
        ks = pl.multiple_of(kt * tk, tk)
        out = []
        for h in range(hp):
            m, l, acc = carry[h]
            s = lax.dot_general(qs[h], k_ref[h, pl.ds(ks, tk), :], (((1,), (1,)), ((), ())),
                                preferred_element_type=F32)
            if masked:
                tpos = lax.broadcasted_iota(jnp.int32, (tq, tk), 0)
                kpos = lax.broadcasted_iota(jnp.int32, (tq, tk), 1)
                s = jnp.where(kpos <= tpos, s, NEG_INF)
            m_new = jnp.maximum(m, jnp.max(s, axis=-1, keepdims=True))
            alpha = jnp.exp2(m - m_new)
            p = jnp.exp2(s - m_new)
            l = alpha * l + jnp.sum(p, axis=-1, keepdims=True)
            acc = alpha * acc + jnp.dot(p.astype(BF16), v_ref[h, pl.ds(ks, tk), :],
                                        preferred_element_type=F32)
            out.append((m_new, l, acc))
        return tuple(out)

    init = tuple((jnp.full((tq, 1), NEG_INF, F32), jnp.zeros((tq, 1), F32),
                  jnp.zeros((tq, dv), F32)) for _ in range(hp))
    carry = lax.fori_loop(0, qi, lambda kt, c: step(kt, c, False), init)
    carry = step(qi, carry, True)
    for h in range(hp):
        _, l, acc = carry[h]
        o_ref[:, h * dv:(h + 1) * dv] = (acc / l).astype(o_ref.dtype)


def _mla_attention(q, k, v, *, tq=512, tk=512):
    b, h, s, dqk = q.shape
    dv = v.shape[-1]
    hp = MLA_HEADS_PER_STEP
    tq = min(tq, s)
    tk = min(tk, tq)
    assert s % tq == 0 and tq == tk and h % hp == 0
    return pl.pallas_call(
        functools.partial(_mla_body, tq=tq, tk=tk),
        out_shape=jax.ShapeDtypeStruct((b, s, h * dv), BF16),
        grid=(b, h // hp, s // tq),
        in_specs=[pl.BlockSpec((None, hp, tq, dqk), lambda bi, hi, qi: (bi, hi, qi, 0)),
                  pl.BlockSpec((None, hp, s, dqk), lambda bi, hi, qi: (bi, hi, 0, 0)),
                  pl.BlockSpec((None, hp, s, dv), lambda bi, hi, qi: (bi, hi, 0, 0))],
        out_specs=pl.BlockSpec((None, tq, hp * dv), lambda bi, hi, qi: (bi, qi, hi)),
        compiler_params=_params(("parallel", "parallel", "arbitrary")),
        name='mla_attention',
    )(q, k, v)


SUBLANES = 8


def _compress_body(x_ref, pos_ref, w1_ref, w2_ref, o_ref, sh_ref):
    n, half = x_ref.shape
    x = x_ref[...].astype(F32)
    first = jnp.dot((x + pos_ref[0:1, :]).astype(BF16), w1_ref[0:half, :],
                    preferred_element_type=F32)
    second = jnp.dot((x + pos_ref[1:2, :]).astype(BF16), w1_ref[half:2 * half, :],
                     preferred_element_type=F32)
    sh_ref[0:n, :] = second
    sh_ref[n:n + SUBLANES, :] = jnp.zeros((SUBLANES, sh_ref.shape[-1]), F32)
    hid = _silu(first + sh_ref[1:n + 1, :])
    o_ref[...] = jnp.dot(hid.astype(BF16), w2_ref[...], preferred_element_type=F32)


def _compress(t, pos_emb, w1, w2):
    assert CMP_LEN == 2 * CMP_STRIDE
    b, s, g, d = t.shape
    n_chunk = s // CMP_STRIDE
    half = CMP_STRIDE * d
    ch = t.reshape(b, n_chunk, CMP_STRIDE, g, d).transpose(0, 3, 1, 2, 4).reshape(b, g, n_chunk, half)
    w2p = jnp.zeros((CMP_HIDDEN, LANES), BF16).at[:, :d].set(w2.astype(BF16))
    out = pl.pallas_call(
        _compress_body,
        out_shape=jax.ShapeDtypeStruct((b, g, n_chunk, LANES), F32),
        grid=(b, g),
        in_specs=[pl.BlockSpec((None, None, n_chunk, half), lambda bi, gi: (bi, gi, 0, 0)),
                  pl.BlockSpec((2, half), lambda bi, gi: (0, 0)),
                  pl.BlockSpec((2 * half, CMP_HIDDEN), lambda bi, gi: (0, 0)),
                  pl.BlockSpec((CMP_HIDDEN, LANES), lambda bi, gi: (0, 0))],
        out_specs=pl.BlockSpec((None, None, n_chunk, LANES), lambda bi, gi: (bi, gi, 0, 0)),
        scratch_shapes=[pltpu.VMEM((n_chunk + SUBLANES, CMP_HIDDEN), F32)],
        compiler_params=_params(("parallel", "parallel")),
        name='nsa_compress',
    )(ch, pos_emb.reshape(2, half), w1.astype(BF16), w2p)
    return out[..., :d]


MASK_BIG = 1e30
NSA_GROUPS_PER_STEP = 1


def _nsa_body(q_ref, gl_ref, kct_ref, vc_ref, kse_ref, vs_ref, kwt_ref, vw_ref, o_ref, *,
              tq, tk, n_blocks, n_top, span):
    hg = NSA_HPG
    d = NSA_HEAD_DIM
    rows = hg * tq
    qi = pl.program_id(2)
    q0 = qi * tq
    gp = kct_ref.shape[0]
    qw = hg * d
    t_q = q0 + lax.broadcasted_iota(jnp.int32, (tq, 1), 0)

    def add_bias(s, bias):
        n = s.shape[-1]
        return (s.reshape(hg, tq, n) + bias[None]).reshape(rows, n)

    n_cmp = kct_ref.shape[-1]
    cmp_end = lax.broadcasted_iota(jnp.int32, (1, n_cmp), 1) * CMP_STRIDE + (CMP_LEN - 1)
    bias_c = jnp.where(cmp_end <= t_q, 0.0, NEG_INF)
    has_c = jnp.concatenate([t_q >= CMP_LEN - 1] * hg, axis=0)
    blk =lax.broadcasted_iota(jnp.int32, (LANES, LANES), 0)
    blk_f = blk.astype(F32)

    def before_loop(g):
        qf = q_ref[:, g * qw:(g + 1) * qw].astype(F32) * (d ** -0.5 * LOG2E)
        q = jnp.concatenate([qf[:, h * d:(h + 1) * d] for h in range(hg)], axis=0).astype(BF16)
        s_c = add_bias(jnp.dot(q, kct_ref[g], preferred_element_type=F32), bias_c)
        e_c = jnp.exp2(s_c - jnp.max(s_c, axis=-1, keepdims=True))
        acc_c = jnp.dot(e_c.astype(BF16), vc_ref[g], preferred_element_type=F32)
        inv_c = jnp.where(has_c, 1.0 / jnp.maximum(acc_c[:, d:d + 1], TINY), 0.0)
        imp = jnp.sum((acc_c[:, LANES:] * inv_c).reshape(hg, tq, LANES), axis=0)
        sel_parts = []
        for r in range(tq // LANES):
            imp_t = imp[r * LANES:(r + 1) * LANES].T
            cur = (q0 + r * LANES + lax.broadcasted_iota(jnp.int32, (1, LANES), 1)) // SLC_LEN
            forced = (blk == 0) | (blk == cur) | (blk == cur - 1)
            causal = blk <= cur
            imp_t = jnp.where(causal & jnp.logical_not(forced), imp_t, -2.0)
            sel_t = jnp.zeros((LANES, LANES), F32)
            for _ in range(n_top - N_FORCED):
                top = jnp.max(imp_t, axis=0, keepdims=True)
                first = jnp.min(jnp.where(imp_t == top, blk_f, float(LANES)), axis=0, keepdims=True)
                hit = blk_f == first
                sel_t = jnp.where(hit, 1.0, sel_t)
                imp_t = jnp.where(hit, -2.0, imp_t)
            sel_t = jnp.where(cur >= n_top, jnp.where(forced, 1.0, sel_t),
                              jnp.where(causal, 1.0, 0.0))
            sel_parts.append(sel_t.T)
        sel = jnp.concatenate(sel_parts, axis=0)
        unsel = jnp.concatenate([sel - 1.0] * hg, axis=0)
        q_aug = jnp.concatenate([unsel.astype(BF16), q], axis=1)
        return q, q_aug, acc_c, inv_c

    pre = [before_loop(g) for g in range(gp)]
    t_rel = t_q - lax.broadcasted_iota(jnp.int32, (1, tk), 1)

    def sel_tile(kt, carry, causal):
        ks = pl.multiple_of(kt * tk, tk)
        bias = jnp.where(t_rel >= ks, 0.0, NEG_INF) if causal else None
        out = []
        for g in range(gp):
            m, acc = carry[g]
            s = jnp.dot(pre[g][1], kse_ref[g, :, pl.ds(ks, tk)], preferred_element_type=F32)
            if causal:
                s = add_bias(s, bias)
            m_new = jnp.maximum(m, jnp.max(s, axis=-1, keepdims=True))
            p = jnp.exp2(s - m_new)
            acc = jnp.exp2(m - m_new) * acc + jnp.dot(p.astype(BF16), vs_ref[g, pl.ds(ks, tk), :],
                                                      preferred_element_type=F32)
            out.append((m_new, acc))
        return tuple(out)

    init = tuple((jnp.full((rows, 1), NEG_INF, F32), jnp.zeros((rows, vs_ref.shape[-1]), F32))
                 for _ in range(gp))
    n_kt = (q0 + tq + tk - 1) // tk
    carry = lax.fori_loop(0, n_kt - 1, lambda kt, c: sel_tile(kt, c, False), init)
    carry = sel_tile(n_kt - 1, carry, True)

    ws = pl.multiple_of(jnp.maximum(q0 + tq - span, 0), LANES)
    dist = (t_q - ws) - lax.broadcasted_iota(jnp.int32, (1, span), 1)
    bias_w = jnp.where((dist >= 0) & (dist < WIN), 0.0, NEG_INF)

    def after_loop(g):
        q, _, acc_c, inv_c = pre[g]
        acc_s = carry[g][1]
        inv_s = 1.0 / jnp.maximum(acc_s[:, d:d + 1], TINY)
        s_w = add_bias(jnp.dot(q, kwt_ref[g, :, pl.ds(ws, span)], preferred_element_type=F32),
                       bias_w)
        e_w = jnp.exp2(s_w - jnp.max(s_w, axis=-1, keepdims=True))
        acc_w = jnp.dot(e_w.astype(BF16), vw_ref[g, pl.ds(ws, span), :],
                        preferred_element_type=F32)
        inv_w = 1.0 / acc_w[:, d:d + 1]
        gates = _sigmoid(gl_ref[g])
        g_t = []
        for r in range(tq // LANES):
            blk_g = jnp.concatenate([gates[:, r * LANES:(r + 1) * LANES],
                                     jnp.zeros((LANES - gates.shape[0], LANES), F32)], axis=0)
            g_t.append(blk_g.T)
        g_t = jnp.concatenate(g_t, axis=0)

        def gate_col(branch):
            return jnp.concatenate([g_t[:, branch * hg + h:branch * hg + h + 1] for h in range(hg)],
                                   axis=0)

        o = ((gate_col(0) * inv_c) * acc_c[:, :d] + (gate_col(1) * inv_s) * acc_s[:, :d]
             + (gate_col(2) * inv_w) * acc_w[:, :d])
        o_ref[:, g * qw:(g + 1) * qw] = jnp.concatenate(
            [o[h * tq:(h + 1) * tq] for h in range(hg)], axis=1).astype(o_ref.dtype)

    for g in range(gp):
        after_loop(g)


def _nsa_attention(proj, gl_t, kct, vc, kst, vs, kwt, vw, *, tq=512, tk=512):
    b, s, _ = proj.shape
    g = NSA_GROUPS
    d = NSA_HEAD_DIM
    n_cmp = kct.shape[-1]
    tq = min(tq, s)
    tk = min(tk, s)
    span = min(WIN + tq, s)
    n_blocks = s // SLC_LEN
    gp = NSA_GROUPS_PER_STEP
    assert s % tq == 0 and s % tk == 0 and tq % LANES == 0 and n_blocks <= LANES and g % gp == 0
    kv_t = lambda n, r=d: pl.BlockSpec((None, gp, r, n), lambda bi, gi, qi: (bi, gi, 0, 0))
    kv_r = lambda n, w=LANES: pl.BlockSpec((None, gp, n, w), lambda bi, gi, qi: (bi, gi, 0, 0))

    def with_ones(v):
        tail = jnp.zeros(v.shape[:-1] + (LANES - d,), BF16).at[..., 0].set(1.0)
        return jnp.concatenate([v, tail], axis=-1)

    blk_of_key = jnp.arange(s, dtype=jnp.int32) // SLC_LEN
    mask_rows = jnp.where(jnp.arange(LANES, dtype=jnp.int32)[:, None] == blk_of_key[None, :],
                          MASK_BIG, 0.0).astype(BF16)
    kse = jnp.concatenate([jnp.broadcast_to(mask_rows, (b, g, LANES, s)), kst], axis=2)
    per = SLC_LEN // CMP_STRIDE
    jj = jnp.arange(n_cmp, dtype=jnp.int32)[:, None]
    nn = jnp.arange(LANES, dtype=jnp.int32)[None, :]
    pool = (0.5 * ((jj // per == nn).astype(F32) + ((jj + 1) // per == nn).astype(F32))).astype(BF16)
    vc = jnp.concatenate([with_ones(vc), jnp.broadcast_to(pool, (b, g, n_cmp, LANES))], axis=-1)
    vs, vw = with_ones(vs), with_ones(vw)
    kst = kse
    return pl.pallas_call(
        functools.partial(_nsa_body, tq=tq, tk=tk, n_blocks=n_blocks,
                          n_top=min(SLC_TOPN, n_blocks), span=span),
        out_shape=jax.ShapeDtypeStruct((b, s, g * NSA_HPG * d), BF16),
        grid=(b, g // gp, s // tq),
        in_specs=[pl.BlockSpec((None, tq, gp * NSA_HPG * d), lambda bi, gi, qi: (bi, qi, gi)),
                  pl.BlockSpec((None, gp, 16, tq), lambda bi, gi, qi: (bi, gi, 0, qi)),
                  kv_t(n_cmp), kv_r(n_cmp, 2 * LANES), kv_t(s, LANES + d), kv_r(s), kv_t(s),
                  kv_r(s)],
        out_specs=pl.BlockSpec((None, tq, gp * NSA_HPG * d), lambda bi, gi, qi: (bi, qi, gi)),
        compiler_params=_params(("parallel", "parallel", "arbitrary")),
        name='nsa_attention',
    )(proj, gl_t, kct, vc, kst, vs, kwt, vw)


def _pad_cols(w, n):
    return jnp.pad(w, ((0, 0), (0, n - w.shape[1])))


def _even_token_mixer(x, mod, norm_g, cos, sin, w_in, conv_w, a_log, dt_bias, dn_norm, q_norm,
                      kv_norm, w_uq, w_ukv, w_out, b, s):
    shift, scale, gate = mod
    n = b * s
    hd = DN_HEADS
    o_b = 4 * DN_W
    o_a = o_b + hd
    o_cq = o_a + hd
    o_ckv = o_cq + MLA_Q_RANK
    o_kr = o_ckv + MLA_KV_RANK
    w_cols = [w_in[:, :o_b], w_in[:, o_cq:o_ckv], w_in[:, o_ckv:o_kr],
              _pad_cols(w_in[:, o_kr:o_kr + MLA_ROPE], LANES),
              jnp.repeat(w_in[:, o_b:o_a], DN_HEAD_DIM, axis=1),
              jnp.repeat(w_in[:, o_a:o_cq], DN_HEAD_DIM, axis=1)]
    w_all = jnp.concatenate(w_cols, axis=1)
    proj = _mm(x, w_all, pro='adaln', pro_args=(norm_g, scale, shift), rows_per_batch=s,
               tn_cap=4096, name='even_in_proj')
    width = proj.shape[1]
    c0 = 4 * DN_W
    pb_blk = (c0 + MLA_Q_RANK + MLA_KV_RANK + LANES) // LANES
    proj3 = proj.reshape(b, s, width)

    o_dn = _deltanet(proj3, conv_w.astype(F32), a_log, dt_bias, dn_norm, z_blk=3 * hd,
                     pb_blk=pb_blk, pa_blk=pb_blk + hd)

    reps = LANES // (MLA_ROPE // 2)
    cos_t = jnp.tile(cos.reshape(n, -1), (1, reps))
    sin_t = jnp.tile(sin.reshape(n, -1), (1, reps))
    q_h = _mla_q_up(proj, q_norm, w_uq, cos_t, sin_t, b, s, cq_blk=c0 // MLA_Q_RANK)
    k_h, v_h = _mla_kv_up(proj, kv_norm, w_ukv, cos_t, sin_t, b, s,
                          ckv_blk=(c0 + MLA_Q_RANK) // LANES,
                          kr_blk=(c0 + MLA_Q_RANK + MLA_KV_RANK) // LANES)
    o_mla = _mla_attention(q_h, k_h, v_h)
    mix = jnp.concatenate([o_dn, o_mla], axis=-1).reshape(n, -1)
    return _mm(mix, w_out, epi='residual', epi_args=(x, gate), rows_per_batch=s, name='even_out_proj')


def _odd_token_mixer(x, mod, norm_g, w_in, pos_k, pos_v, ck1, ck2, cv1, cv2, w_out, b, s):
    shift, scale, gate = mod
    n = b * s
    g = NSA_GROUPS
    d = NSA_HEAD_DIM
    n_in = NSA_Q_W + 6 * NSA_KV_W + 3 * NSA_HEADS
    w_all = _pad_cols(w_in, -(-n_in // LANES) * LANES)
    proj = _mm(x, w_all, pro='adaln', pro_args=(norm_g, scale, shift), rows_per_batch=s,
               tn_cap=4096, out_dtype=BF16, name='odd_in_proj')
    proj3 = proj.reshape(b, s, -1)

    def kv(i):
        lo = NSA_Q_W + i * NSA_KV_W
        return proj3[:, :, lo:lo + NSA_KV_W].reshape(b, s, g, d)

    k_cmp = _compress(kv(0), pos_k, ck1, ck2)
    v_cmp = _compress(kv(1), pos_v, cv1, cv2)
    to_t = lambda t: t.transpose(0, 2, 3, 1).astype(BF16)
    to_r = lambda t: t.transpose(0, 2, 1, 3).astype(BF16)
    gl = proj3[:, :, NSA_Q_W + 6 * NSA_KV_W:n_in].reshape(b, s, g, NSA_HPG, 3)
    gl_t = gl.transpose(0, 2, 4, 3, 1).reshape(b, g, 3 * NSA_HPG, s)
    gl_t = jnp.pad(gl_t.astype(F32), ((0, 0), (0, 0), (0, 16 - 3 * NSA_HPG), (0, 0)))
    o = _nsa_attention(proj3, gl_t, k_cmp.transpose(0, 1, 3, 2).astype(BF16), v_cmp.astype(BF16),
                       to_t(kv(2)), to_r(kv(3)), to_t(kv(4)), to_r(kv(5)))
    return _mm(o.reshape(n, -1), w_out, epi='residual', epi_args=(x, gate), rows_per_batch=s,
               name='odd_out_proj')


def _moe(x, mod, norm_g, w_router, b_router, w1, w3, w2, layer, b, s):
    shift, scale, gate = mod
    n = b * s
    h, logits = _norm_router(x, norm_g, scale, shift, w_router, s)
    logits = logits[:, :N_EXPERTS] + b_router.astype(F32)
    top_val, top_idx = lax.top_k(logits, TOP_K)
    gate_w = jax.nn.softmax(top_val, axis=-1)
    flat_e = top_idx.reshape(-1)
    onehot = (flat_e[:, None] == jnp.arange(N_EXPERTS, dtype=jnp.int32)[None, :]).astype(jnp.int32)
    rank = jnp.take_along_axis(jnp.cumsum(onehot, axis=0), flat_e[:, None], axis=1)[:, 0] - 1
    counts = jnp.sum(onehot, axis=0)
    padded = ((counts + MOE_BLOCK - 1) // MOE_BLOCK) * MOE_BLOCK
    pad_end = jnp.cumsum(padded)
    pad_start = pad_end - padded
    dest = pad_start[flat_e] + rank
    n_assign = n * TOP_K
    n_blk = -(-n_assign // MOE_BLOCK) + N_EXPERTS
    n_rows = n_blk * MOE_BLOCK
    row_assign = jnp.zeros((n_rows,), jnp.int32).at[dest].set(
        jnp.arange(n_assign, dtype=jnp.int32), unique_indices=True)
    row_tok = row_assign // TOP_K
    blk_exp = jnp.minimum(jnp.searchsorted(pad_end, jnp.arange(n_blk, dtype=jnp.int32) * MOE_BLOCK,
                                           side='right'), N_EXPERTS - 1).astype(jnp.int32)
    n_used = (pad_end[-1:] // MOE_BLOCK).astype(jnp.int32)
    row_w = gate_w.reshape(-1)[row_assign]
    y_rows = _swiglu_grouped(h[row_tok], row_w[:, None], blk_exp, n_used, w1, w3, w2, layer)
    dest_tok = dest.reshape(n, TOP_K)
    return _moe_combine(x, [y_rows[dest_tok[:, j]] for j in range(TOP_K)], gate, s)


def kernel(x, c, positions, ada_w, ada_b, norm_g, final_g, ev_w_in, ev_conv_w, ev_a_log, ev_dt_bias, ev_dn_norm, ev_q_norm, ev_kv_norm, ev_w_uq, ev_w_ukv, ev_w_out, ev_ff_gate, ev_ff_up, ev_ff_down, od_w_in, od_cmp_pos_k, od_cmp_pos_v, od_cmp_k1, od_cmp_k2, od_cmp_v1, od_cmp_v2, od_w_out, od_router, od_router_b, od_moe_w1, od_moe_w3, od_moe_w2):
    b, s, dm = x.shape
    depth = ada_w.shape[0]
    inv = 1.0 / (ROPE_THETA ** (jnp.arange(0, MLA_ROPE, 2, dtype=F32) / MLA_ROPE))
    ang = positions.astype(F32)[..., None] * inv
    cos, sin = jnp.cos(ang), jnp.sin(ang)
    mods = jnp.einsum('bd,lkde->lkbe', jax.nn.silu(c), ada_w,
                      precision=lax.Precision.HIGHEST) + ada_b[:, :, None, :]

    def mod(layer, k):
        m = mods[layer, k]
        return tuple(m[:, None, i * dm:(i + 1) * dm] for i in range(3))

    xf = x.reshape(b * s, dm)
    for layer in range(depth):
        j = layer // 2
        if layer % 2 == 0:
            xf = _even_token_mixer(xf, mod(layer, 0), norm_g[layer, 0], cos, sin, ev_w_in[j],
                                   ev_conv_w[j], ev_a_log[j], ev_dt_bias[j], ev_dn_norm[j],
                                   ev_q_norm[j], ev_kv_norm[j], ev_w_uq[j], ev_w_ukv[j],
                                   ev_w_out[j], b, s)
            shift, scale, gate = mod(layer, 1)
            xf = _swiglu_dense(xf, norm_g[layer, 1], scale, shift, ev_ff_gate[j], ev_ff_up[j],
                               ev_ff_down[j], gate, s)
        else:
            xf = _odd_token_mixer(xf, mod(layer, 0), norm_g[layer, 0], od_w_in[j],
                                  od_cmp_pos_k[j], od_cmp_pos_v[j], od_cmp_k1[j], od_cmp_k2[j],
                                  od_cmp_v1[j], od_cmp_v2[j], od_w_out[j], b, s)
            xf = _moe(xf, mod(layer, 1), norm_g[layer, 1], od_router[j], od_router_b[j],
                      od_moe_w1, od_moe_w3, od_moe_w2, j, b, s)
    return _final_norm(xf, final_g).reshape(b, s, dm)
```

```python
import functools

import jax
import jax.numpy as jnp
import numpy as np
from jax import lax
from jax.experimental import pallas as pl
from jax.experimental.pallas import tpu as pltpu

F32 = jnp.float32
BF16 = jnp.bfloat16

NORM_EPS = 1e-6
NEG_INF = -1e30
TINY = 1e-30

LANES = 128

DN_HEADS = 4
DN_HEAD_DIM = 128
DN_CONV = 4
DN_CHUNK = 64
DN_W = DN_HEADS * DN_HEAD_DIM

MLA_HEADS = 4
MLA_Q_RANK = 256
MLA_KV_RANK = 128
MLA_NOPE = 128
MLA_ROPE = 64
MLA_V = 128
MLA_QK = MLA_NOPE + MLA_ROPE
ROPE_THETA = 10000.0

NSA_HEADS = 16
NSA_GROUPS = 4
NSA_HPG = NSA_HEADS // NSA_GROUPS
NSA_HEAD_DIM = 64
NSA_Q_W = NSA_HEADS * NSA_HEAD_DIM
NSA_KV_W = NSA_GROUPS * NSA_HEAD_DIM
CMP_LEN = 32
CMP_STRIDE = 16
CMP_HIDDEN = 256
SLC_LEN = 64
SLC_TOPN = 16
WIN = 512
N_FORCED = 3

N_EXPERTS = 8
TOP_K = 2
MOE_BLOCK = 512

VMEM_LIMIT = 56 * 1024 * 1024


def _params(sem):
    return pltpu.CompilerParams(dimension_semantics=sem, vmem_limit_bytes=VMEM_LIMIT)


def _sigmoid(x):
    return 1.0 / (1.0 + jnp.exp(-x))


def _silu(x):
    return x * _sigmoid(x)


def _dot(a, b):
    return jnp.dot(a.astype(BF16), b.astype(BF16), preferred_element_type=F32)


def _dot_nt(a, b):
    return lax.dot_general(a.astype(BF16), b.astype(BF16), (((1,), (1,)), ((), ())),
                           preferred_element_type=F32)


def _dot_tn(a, b):
    return lax.dot_general(a.astype(BF16), b.astype(BF16), (((0,), (0,)), ((), ())),
                           preferred_element_type=F32)


def _split3(x):
    hi = x.astype(BF16)
    r1 = x - hi.astype(F32)
    mid = r1.astype(BF16)
    lo = (r1 - mid.astype(F32)).astype(BF16)
    return hi, mid, lo


def _dot_lhs01(a01, x):
    hi, mid, lo = _split3(x)
    a = a01.astype(BF16)
    d = functools.partial(jnp.dot, preferred_element_type=F32)
    return d(a, hi) + d(a, mid) + d(a, lo)


def _rms(x, gain):
    return x * lax.rsqrt(jnp.mean(x * x, axis=-1, keepdims=True) + NORM_EPS) * gain


def _mm_body(*refs, pro, epi):
    x_ref, w_ref = refs[0], refs[1]
    pos = 2
    if pro == 'adaln':
        g_ref, sc_ref, sh_ref = refs[pos:pos + 3]
        pos += 3
    elif pro == 'rms':
        g_ref = refs[pos]
        pos += 1
    if epi == 'residual':
        res_ref, gate_ref = refs[pos:pos + 2]
        pos += 2
    o_ref = refs[pos]
    pos += 1
    if pro is not None:
        h_ref = refs[pos]

        @pl.when(pl.program_id(1) == 0)
        def _():
            y = _rms(x_ref[...].astype(F32), g_ref[...])
            if pro == 'adaln':
                y = y * (1.0 + sc_ref[...]) + sh_ref[...]
            h_ref[...] = y.astype(BF16)

        h = h_ref[...]
    else:
        h = x_ref[...].astype(BF16)
    acc = jnp.dot(h, w_ref[...], preferred_element_type=F32)
    if epi == 'residual':
        acc = res_ref[...] + gate_ref[...] * acc
    o_ref[...] = acc.astype(o_ref.dtype)


def _pick_tile(n, cap):
    best = LANES
    t = LANES
    while t <= min(n, cap):
        if n % t == 0:
            best = t
        t += LANES
    return best


def _mm(x, w, *, pro=None, pro_args=(), epi=None, epi_args=(), rows_per_batch=None,
        out_dtype=F32, tm=512, tn_cap=1024, name='mm'):
    m, k = x.shape
    n = w.shape[1]
    tm = min(tm, m)
    tn = _pick_tile(n, tn_cap)
    assert m % tm == 0 and n % tn == 0
    rpb = None if rows_per_batch is None else rows_per_batch // tm
    in_specs = [pl.BlockSpec((tm, k), lambda i, j: (i, 0)),
                pl.BlockSpec((k, tn), lambda i, j: (0, j))]
    args = [x, w.astype(BF16)]
    scratch = []
    if pro == 'adaln':
        g, sc, sh = pro_args
        in_specs += [pl.BlockSpec((1, k), lambda i, j: (0, 0)),
                     pl.BlockSpec((None, 1, k), lambda i, j: (i // rpb, 0, 0)),
                     pl.BlockSpec((None, 1, k), lambda i, j: (i // rpb, 0, 0))]
        args += [g.reshape(1, k), sc, sh]
    elif pro == 'rms':
        in_specs += [pl.BlockSpec((1, k), lambda i, j: (0, 0))]
        args += [pro_args[0].reshape(1, k)]
    if pro is not None:
        scratch = [pltpu.VMEM((tm, k), BF16)]
    if epi == 'residual':
        res, gate = epi_args
        in_specs += [pl.BlockSpec((tm, tn), lambda i, j: (i, j)),
                     pl.BlockSpec((None, 1, tn), lambda i, j: (i // rpb, 0, j))]
        args += [res, gate]
    return pl.pallas_call(
        functools.partial(_mm_body, pro=pro, epi=epi),
        out_shape=jax.ShapeDtypeStruct((m, n), out_dtype),
        grid=(m // tm, n // tn),
        in_specs=in_specs,
        out_specs=pl.BlockSpec((tm, tn), lambda i, j: (i, j)),
        scratch_shapes=scratch,
        compiler_params=_params(("parallel", "arbitrary")),
        name=name,
    )(*args)


def _swiglu_dense_body(x_ref, g_ref, sc_ref, sh_ref, wg_ref, wu_ref, wd_ref, gate_ref, o_ref,
                       h_ref, acc_ref):
    j = pl.program_id(1)

    @pl.when(j == 0)
    def _():
        y = _rms(x_ref[...], g_ref[...]) * (1.0 + sc_ref[...]) + sh_ref[...]
        h_ref[...] = y.astype(BF16)
        acc_ref[...] = jnp.zeros_like(acc_ref)

    h = h_ref[...]
    a = jnp.dot(h, wg_ref[...], preferred_element_type=F32)
    b = jnp.dot(h, wu_ref[...], preferred_element_type=F32)
    act = (_silu(a) * b).astype(BF16)
    acc_ref[...] += jnp.dot(act, wd_ref[...], preferred_element_type=F32)

    @pl.when(j == pl.num_programs(1) - 1)
    def _():
        o_ref[...] = x_ref[...] + gate_ref[...] * acc_ref[...]


def _swiglu_dense(x, g, sc, sh, wg, wu, wd, gate, rows_per_batch, *, tm=1024, tf_cap=256):
    m, k = x.shape
    f = wg.shape[1]
    tf = _pick_tile(f, tf_cap)
    assert m % tm == 0 and f % tf == 0
    rpb = rows_per_batch // tm
    return pl.pallas_call(
        _swiglu_dense_body,
        out_shape=jax.ShapeDtypeStruct((m, k), F32),
        grid=(m // tm, f // tf),
        in_specs=[pl.BlockSpec((tm, k), lambda i, j: (i, 0)),
                  pl.BlockSpec((1, k), lambda i, j: (0, 0)),
                  pl.BlockSpec((None, 1, k), lambda i, j: (i // rpb, 0, 0)),
                  pl.BlockSpec((None, 1, k), lambda i, j: (i // rpb, 0, 0)),
                  pl.BlockSpec((k, tf), lambda i, j: (0, j)),
                  pl.BlockSpec((k, tf), lambda i, j: (0, j)),
                  pl.BlockSpec((tf, k), lambda i, j: (j, 0)),
                  pl.BlockSpec((None, 1, k), lambda i, j: (i // rpb, 0, 0))],
        out_specs=pl.BlockSpec((tm, k), lambda i, j: (i, 0)),
        scratch_shapes=[pltpu.VMEM((tm, k), BF16), pltpu.VMEM((tm, k), F32)],
        compiler_params=_params(("parallel", "arbitrary")),
        name='swiglu_dense',
    )(x, g.reshape(1, k), sc, sh, wg.astype(BF16), wu.astype(BF16), wd.astype(BF16), gate)


def _swiglu_grouped_body(be_ref, nu_ref, x_ref, rw_ref, wg_ref, wu_ref, wd_ref, o_ref, acc_ref):
    i = pl.program_id(0)
    j = pl.program_id(1)
    used = i < nu_ref[0]

    @pl.when(j == 0)
    def _():
        acc_ref[...] = jnp.zeros_like(acc_ref)

    @pl.when(used)
    def _():
        h = x_ref[...]
        a = jnp.dot(h, wg_ref[...], preferred_element_type=F32)
        b = jnp.dot(h, wu_ref[...], preferred_element_type=F32)
        act = (_silu(a) * b).astype(BF16)
        acc_ref[...] += jnp.dot(act, wd_ref[...], preferred_element_type=F32)

    @pl.when(j == pl.num_programs(1) - 1)
    def _():
        o_ref[...] = (acc_ref[...] * rw_ref[...]).astype(o_ref.dtype)


def _swiglu_grouped(x_rows, row_w, blk_exp, n_used, w1, w3, w2, layer, *, tm=MOE_BLOCK,
                    tf_cap=1792):
    n_rows, k = x_rows.shape
    f = w1.shape[3]
    tf = _pick_tile(f, tf_cap)
    assert n_rows % tm == 0 and f % tf == 0
    nj = f // tf

    def jj(i, j, nu):
        return jnp.where(i < nu[0], j, nj - 1)

    grid_spec = pltpu.PrefetchScalarGridSpec(
        num_scalar_prefetch=2,
        grid=(n_rows // tm, nj),
        in_specs=[pl.BlockSpec((tm, k), lambda i, j, be, nu: (i, 0)),
                  pl.BlockSpec((tm, 1), lambda i, j, be, nu: (i, 0)),
                  pl.BlockSpec((None, None, k, tf),
                               lambda i, j, be, nu: (layer, be[i], 0, jj(i, j, nu))),
                  pl.BlockSpec((None, None, k, tf),
                               lambda i, j, be, nu: (layer, be[i], 0, jj(i, j, nu))),
                  pl.BlockSpec((None, None, tf, k),
                               lambda i, j, be, nu: (layer, be[i], jj(i, j, nu), 0))],
        out_specs=pl.BlockSpec((tm, k), lambda i, j, be, nu: (i, 0)),
        scratch_shapes=[pltpu.VMEM((tm, k), F32)],
    )
    return pl.pallas_call(
        _swiglu_grouped_body,
        out_shape=jax.ShapeDtypeStruct((n_rows, k), BF16),
        grid_spec=grid_spec,
        compiler_params=_params(("arbitrary", "arbitrary")),
        name='swiglu_grouped',
    )(blk_exp, n_used, x_rows, row_w, w1, w3, w2)


def _cast_body(x_ref, o_ref):
    o_ref[...] = x_ref[...].astype(o_ref.dtype)


def _cast_bf16(w, *, block_elems=2 * 1024 * 1024):
    shape = w.shape
    cols = shape[-1]
    w2 = w.reshape(-1, cols)
    rows = w2.shape[0]
    tm = 1 << ((block_elems // cols).bit_length() - 1)
    assert rows % tm == 0 and cols % LANES == 0
    out = pl.pallas_call(
        _cast_body,
        out_shape=jax.ShapeDtypeStruct((rows, cols), BF16),
        grid=(rows // tm,),
        in_specs=[pl.BlockSpec((tm, cols), lambda i: (i, 0))],
        out_specs=pl.BlockSpec((tm, cols), lambda i: (i, 0)),
        compiler_params=_params(("parallel",)),
        name='cast_bf16',
    )(w2)
    return out.reshape(shape)


def _moe_combine_body(x_ref, gate_ref, *refs):
    y_refs, o_ref = refs[:-1], refs[-1]
    total = y_refs[0][...].astype(F32)
    for y_ref in y_refs[1:]:
        total = total + y_ref[...].astype(F32)
    o_ref[...] = x_ref[...] + gate_ref[...] * total


def _moe_combine(x, y_slots, gate, rows_per_batch, *, tm=512):
    m, k = x.shape
    rpb = rows_per_batch // tm
    row_blk = pl.BlockSpec((tm, k), lambda i: (i, 0))
    return pl.pallas_call(
        _moe_combine_body,
        out_shape=jax.ShapeDtypeStruct((m, k), F32),
        grid=(m // tm,),
        in_specs=[row_blk, pl.BlockSpec((None, 1, k), lambda i: (i // rpb, 0, 0))]
                 + [row_blk] * len(y_slots),
        out_specs=row_blk,
        compiler_params=_params(("parallel",)),
        name='moe_combine',
    )(x, gate, *y_slots)


def _norm_router_body(x_ref, g_ref, sc_ref, sh_ref, wr_ref, h_ref, lg_ref):
    y = _rms(x_ref[...], g_ref[...]) * (1.0 + sc_ref[...]) + sh_ref[...]
    h_ref[...] = y.astype(BF16)
    lg_ref[...] = jnp.dot(y, wr_ref[...], preferred_element_type=F32,
                          precision=lax.Precision.HIGHEST)


def _norm_router(x, g, sc, sh, w_router, rows_per_batch, *, tm=512):
    m, k = x.shape
    e = w_router.shape[1]
    wr = jnp.zeros((k, LANES), F32).at[:, :e].set(w_router)
    rpb = rows_per_batch // tm
    return pl.pallas_call(
        _norm_router_body,
        out_shape=(jax.ShapeDtypeStruct((m, k), BF16), jax.ShapeDtypeStruct((m, LANES), F32)),
        grid=(m // tm,),
        in_specs=[pl.BlockSpec((tm, k), lambda i: (i, 0)),
                  pl.BlockSpec((1, k), lambda i: (0, 0)),
                  pl.BlockSpec((None, 1, k), lambda i: (i // rpb, 0, 0)),
                  pl.BlockSpec((None, 1, k), lambda i: (i // rpb, 0, 0)),
                  pl.BlockSpec((k, LANES), lambda i: (0, 0))],
        out_specs=(pl.BlockSpec((tm, k), lambda i: (i, 0)),
                   pl.BlockSpec((tm, LANES), lambda i: (i, 0))),
        compiler_params=_params(("parallel",)),
        name='norm_router',
    )(x, g.reshape(1, k), sc, sh, wr)


def _final_norm_body(x_ref, g_ref, o_ref):
    o_ref[...] = _rms(x_ref[...], g_ref[...])


def _final_norm(x, g, *, tm=1024):
    m, k = x.shape
    return pl.pallas_call(
        _final_norm_body,
        out_shape=jax.ShapeDtypeStruct((m, k), F32),
        grid=(m // tm,),
        in_specs=[pl.BlockSpec((tm, k), lambda i: (i, 0)), pl.BlockSpec((1, k), lambda i: (0, 0))],
        out_specs=pl.BlockSpec((tm, k), lambda i: (i, 0)),
        compiler_params=_params(("parallel",)),
        name='final_norm',
    )(x, g.reshape(1, k))


def _softplus(x):
    return jnp.maximum(x, 0.0) + jnp.log(1.0 + jnp.exp(-jnp.abs(x)))


CONV_HALO = 8
DN_HEADS_PER_STEP = 4


def _deltanet_body(q_ref, k_ref, v_ref, z_ref, pb_ref, pa_ref, cwq_ref, cwk_ref, cwv_ref, alog_ref,
                   dtb_ref, gn_ref, o_ref, s_ref, xs_ref, *, n_chunks):
    c_len = DN_CHUNK
    tile = n_chunks * c_len

    @pl.when(pl.program_id(1) == 0)
    def _():
        s_ref[...] = jnp.zeros_like(s_ref)
        xs_ref[:, 0:CONV_HALO, :] = jnp.zeros((3, CONV_HALO, xs_ref.shape[-1]), F32)

    def conv(i, x_ref, cw_ref):
        xs_ref[i, CONV_HALO:CONV_HALO + tile, :] = x_ref[...]
        y = None
        for tap in range(DN_CONV):
            off = CONV_HALO - (DN_CONV - 1) + tap
            term = xs_ref[i, off:off + tile, :] * cw_ref[tap:tap + 1, :]
            y = term if y is None else y + term
        xs_ref[i, 0:CONV_HALO, :] = xs_ref[i, tile:tile + CONV_HALO, :]
        return _silu(y)

    q = conv(0, q_ref, cwq_ref)
    k = conv(1, k_ref, cwk_ref)
    v = conv(2, v_ref, cwv_ref)

    row = lax.broadcasted_iota(jnp.int32, (c_len, c_len), 0)
    col = lax.broadcasted_iota(jnp.int32, (c_len, c_len), 1)
    incl = row >= col
    strict = row > col
    incl_f = incl.astype(F32)
    strict_f = strict.astype(F32)
    eye = (row == col).astype(F32)
    dk = DN_HEAD_DIM
    hp = q.shape[-1] // dk
    head = lambda x, h: x[:, h * dk:(h + 1) * dk]

    def l2n(x, scale):
        return jnp.concatenate(
            [head(x, h) * (lax.rsqrt(jnp.sum(head(x, h) * head(x, h), axis=-1, keepdims=True)
                                     + NORM_EPS) * scale) for h in range(hp)], axis=1)

    q = l2n(q, dk ** -0.5)
    k = l2n(k, 1.0)
    beta = _sigmoid(pb_ref[...])
    g = -jnp.exp(alog_ref[...]) * _softplus(pa_ref[...] + dtb_ref[...])
    kb = k * beta
    vb = v * beta
    chunks = range(hp * n_chunks)

    def cs(x, i):
        h, c = divmod(i, n_chunks)
        return x[c * c_len:(c + 1) * c_len, h * dk:(h + 1) * dk]
    gg = [_dot_lhs01(incl_f, jnp.concatenate([cs(g, c), cs(g, c)[:, :c_len] * strict_f], axis=1))
          for c in chunks]
    gc = [x[:, :dk] for x in gg]
    decay = [jnp.where(incl, jnp.exp(jnp.where(incl, x[:, dk:dk + c_len], 0.0)), 0.0) for x in gg]
    eg = [jnp.exp(x) for x in gc]
    kq = [_dot_nt(jnp.concatenate([cs(kb, c), cs(q, c)], axis=0), cs(k, c)) for c in chunks]
    lower = [jnp.where(strict, kq[c][:c_len] * decay[c], 0.0) for c in chunks]
    qk = [kq[c][c_len:] * decay[c] for c in chunks]
    tinv = [eye - lo for lo in lower]
    pw = lower
    for _ in range(5):
        pw = [_dot(p, p) for p in pw]
        tinv = [t + _dot(t, p) for t, p in zip(tinv, pw)]
    wu = [_dot(tinv[c], jnp.concatenate([cs(kb, c) * eg[c], cs(vb, c)], axis=1)) for c in chunks]
    g_last = [x[c_len - 1:c_len, :] for x in gc]
    k_tail = [cs(k, c) * jnp.exp(g_last[c] - gc[c]) for c in chunks]
    mb = [_dot_tn(k_tail[c], wu[c]) for c in chunks]
    qo = [_dot(qk[c], wu[c]) for c in chunks]
    lhs = [jnp.concatenate([cs(q, c) * eg[c] - qo[c][:, :dk], mb[c][:, :dk]], axis=0)
           for c in chunks]
    z = z_ref[...]
    for h in range(hp):
        state = s_ref[h]
        outs = []
        for c in range(h * n_chunks, (h + 1) * n_chunks):
            prod = _dot(lhs[c], state)
            outs.append(prod[:c_len] + qo[c][:, dk:])
            state = state * jnp.exp(g_last[c]) - prod[c_len:] + mb[c][:, dk:]
        s_ref[h] = state
        o = jnp.concatenate(outs, axis=0)
        o_ref[:, h * dk:(h + 1) * dk] = (_rms(o, gn_ref[...]) * _silu(head(z, h))).astype(o_ref.dtype)


def _deltanet(proj, conv_w, a_log, dt_bias, dn_norm, *, z_blk, pb_blk, pa_blk, tile=512):
    b, s, _ = proj.shape
    h = DN_HEADS
    d = DN_HEAD_DIM
    tile = min(tile, s)
    hp = DN_HEADS_PER_STEP
    ng = h // hp
    assert s % tile == 0 and tile % DN_CHUNK == 0 and h % hp == 0
    assert all(off % hp == 0 for off in (h, z_blk, pb_blk, pa_blk))
    rep = lambda t: jnp.repeat(t.astype(F32), d).reshape(1, h * d)

    def col(off):
        return pl.BlockSpec((None, tile, hp * d),
                            lambda i, t, off=off: (i // ng, t, off // hp + i % ng))

    def cw(off):
        return pl.BlockSpec((DN_CONV, hp * d), lambda i, t, off=off: (0, off // hp + i % ng))

    per_head = pl.BlockSpec((1, hp * d), lambda i, t: (0, i % ng))
    return pl.pallas_call(
        functools.partial(_deltanet_body, n_chunks=tile // DN_CHUNK),
        out_shape=jax.ShapeDtypeStruct((b, s, h * d), BF16),
        grid=(b * ng, s // tile),
        in_specs=[col(0), col(h), col(2 * h), col(z_blk), col(pb_blk), col(pa_blk),
                  cw(0), cw(h), cw(2 * h),
                  per_head, per_head, pl.BlockSpec((1, d), lambda i, t: (0, 0))],
        out_specs=col(0),
        scratch_shapes=[pltpu.VMEM((hp, d, d), F32),
                        pltpu.VMEM((3, tile + CONV_HALO, hp * d), F32)],
        compiler_params=_params(("parallel", "arbitrary")),
        name='deltanet',
    )(proj, proj, proj, proj, proj, proj, conv_w, conv_w, conv_w, rep(a_log), rep(dt_bias),
      dn_norm.reshape(1, d))


LOG2E = 1.4426950408889634


def _rope_halves(x1, x2, cos, sin):
    return x1 * cos - x2 * sin, x2 * cos + x1 * sin


def _mla_q_up_body(cq_ref, g_ref, w_ref, cos_ref, sin_ref, o_ref):
    hm, half = MLA_HEADS, MLA_ROPE // 2
    x = _rms(cq_ref[...], g_ref[...]).astype(BF16)
    acc = jnp.dot(x, w_ref[...], preferred_element_type=F32)
    nope_w = hm * MLA_NOPE
    r1, r2 = _rope_halves(acc[:, nope_w:nope_w + hm * half], acc[:, nope_w + hm * half:],
                          cos_ref[...], sin_ref[...])
    for h in range(hm):
        q_h = jnp.concatenate([acc[:, h * MLA_NOPE:(h + 1) * MLA_NOPE],
                               r1[:, h * half:(h + 1) * half], r2[:, h * half:(h + 1) * half]], axis=1)
        o_ref[h] = (q_h * (MLA_QK ** -0.5 * LOG2E)).astype(o_ref.dtype)


def _mla_q_up(proj, q_norm, w_uq, cos_t, sin_t, b, s, *, cq_blk, tm=512):
    n = proj.shape[0]
    hm, half = MLA_HEADS, MLA_ROPE // 2
    assert hm * half == LANES and s % tm == 0
    heads = np.arange(hm)[:, None] * MLA_QK
    cols = np.concatenate([(heads + np.arange(MLA_NOPE)[None, :]).ravel(),
                           (heads + MLA_NOPE + np.arange(half)[None, :]).ravel(),
                           (heads + MLA_NOPE + half + np.arange(half)[None, :]).ravel()])
    rpb = s // tm
    return pl.pallas_call(
        _mla_q_up_body,
        out_shape=jax.ShapeDtypeStruct((b, hm, s, MLA_QK), BF16),
        grid=(n // tm,),
        in_specs=[pl.BlockSpec((tm, MLA_Q_RANK), lambda i: (i, cq_blk)),
                  pl.BlockSpec((1, MLA_Q_RANK), lambda i: (0, 0)),
                  pl.BlockSpec((MLA_Q_RANK, hm * MLA_QK), lambda i: (0, 0)),
                  pl.BlockSpec((tm, LANES), lambda i: (i, 0)),
                  pl.BlockSpec((tm, LANES), lambda i: (i, 0))],
        out_specs=pl.BlockSpec((None, hm, tm, MLA_QK), lambda i: (i // rpb, 0, i % rpb, 0)),
        compiler_params=_params(("parallel",)),
        name='mla_q_up',
    )(proj, q_norm.reshape(1, -1), w_uq[:, cols].astype(BF16), cos_t, sin_t)


def _mla_kv_up_body(ckv_ref, kr_ref, g_ref, w_ref, cos_ref, sin_ref, k_ref, v_ref):
    hm, half = MLA_HEADS, MLA_ROPE // 2
    x = _rms(ckv_ref[...], g_ref[...]).astype(BF16)
    acc = jnp.dot(x, w_ref[...], preferred_element_type=F32)
    kr = kr_ref[...]
    r1, r2 = _rope_halves(kr[:, :half], kr[:, half:2 * half], cos_ref[:, :half], sin_ref[:, :half])
    for h in range(hm):
        k_h = jnp.concatenate([acc[:, h * MLA_NOPE:(h + 1) * MLA_NOPE], r1, r2], axis=1)
        k_ref[h] = k_h.astype(k_ref.dtype)
        v_lo = hm * MLA_NOPE + h * MLA_V
        v_ref[h] = acc[:, v_lo:v_lo + MLA_V].astype(v_ref.dtype)


def _mla_kv_up(proj, kv_norm, w_ukv, cos_t, sin_t, b, s, *, ckv_blk, kr_blk, tm=512):
    n = proj.shape[0]
    hm = MLA_HEADS
    assert MLA_KV_RANK == LANES and MLA_V == LANES and s % tm == 0
    heads = np.arange(hm)[:, None] * (MLA_NOPE + MLA_V)
    cols = np.concatenate([(heads + np.arange(MLA_NOPE)[None, :]).ravel(),
                           (heads + MLA_NOPE + np.arange(MLA_V)[None, :]).ravel()])
    rpb = s // tm
    head_major = lambda w: pl.BlockSpec((None, hm, tm, w), lambda i: (i // rpb, 0, i % rpb, 0))
    return pl.pallas_call(
        _mla_kv_up_body,
        out_shape=(jax.ShapeDtypeStruct((b, hm, s, MLA_QK), BF16),
                   jax.ShapeDtypeStruct((b, hm, s, MLA_V), BF16)),
        grid=(n // tm,),
        in_specs=[pl.BlockSpec((tm, LANES), lambda i: (i, ckv_blk)),
                  pl.BlockSpec((tm, LANES), lambda i: (i, kr_blk)),
                  pl.BlockSpec((1, LANES), lambda i: (0, 0)),
                  pl.BlockSpec((MLA_KV_RANK, hm * (MLA_NOPE + MLA_V)), lambda i: (0, 0)),
                  pl.BlockSpec((tm, LANES), lambda i: (i, 0)),
                  pl.BlockSpec((tm, LANES), lambda i: (i, 0))],
        out_specs=(head_major(MLA_QK), head_major(MLA_V)),
        compiler_params=_params(("parallel",)),
        name='mla_kv_up',
    )(proj, proj, kv_norm.reshape(1, -1), w_ukv[:, cols].astype(BF16), cos_t, sin_t)


MLA_HEADS_PER_STEP = 2


def _mla_body(q_ref, k_ref, v_ref, o_ref, *, tq, tk):
    qi = pl.program_id(2)
    hp = q_ref.shape[0]
    dv = v_ref.shape[-1]
    qs = [q_ref[h] for h in range(hp)]

    def step(kt, carry, masked):
        ks = pl.multiple_of(kt * tk, tk)
        out = []
        for h in range(hp):
            m, l, acc = carry[h]
            s = lax.dot_general(qs[h], k_ref[h, pl.ds(ks, tk), :], (((1,), (1,)), ((), ())),
                                preferred_element_type=F32)
            if masked:
                tpos = lax.broadcasted_iota(jnp.int32, (tq, tk), 0)
                kpos = lax.broadcasted_iota(jnp.int32, (tq, tk), 1)
                s = jnp.where(kpos <= tpos, s, NEG_INF)
            m_new = jnp.maximum(m, jnp.max(s, axis=-1, keepdims=True))
            alpha = jnp.exp2(m - m_new)
            p = jnp.exp2(s - m_new)
            l = alpha * l + jnp.sum(p, axis=-1, keepdims=True)
            acc = alpha * acc + jnp.dot(p.astype(BF16), v_ref[h, pl.ds(ks, tk), :],
                                        preferred_element_type=F32)
            out.append((m_new, l, acc))
        return tuple(out)

    init = tuple((jnp.full((tq, 1), NEG_INF, F32), jnp.zeros((tq, 1), F32),
                  jnp.zeros((tq, dv), F32)) for _ in range(hp))
    carry = lax.fori_loop(0, qi, lambda kt, c: step(kt, c, False), init)
    carry = step(qi, carry, True)
    for h in range(hp):
        _, l, acc = carry[h]
        o_ref[:, h * dv:(h + 1) * dv] = (acc / l).astype(o_ref.dtype)


def _mla_attention(q, k, v, *, tq=512, tk=512):
    b, h, s, dqk = q.shape
    dv = v.shape[-1]
    hp = MLA_HEADS_PER_STEP
    tq = min(tq, s)
    tk = min(tk, tq)
    assert s % tq == 0 and tq == tk and h % hp == 0
    return pl.pallas_call(
        functools.partial(_mla_body, tq=tq, tk=tk),
        out_shape=jax.ShapeDtypeStruct((b, s, h * dv), BF16),
        grid=(b, h // hp, s // tq),
        in_specs=[pl.BlockSpec((None, hp, tq, dqk), lambda bi, hi, qi: (bi, hi, qi, 0)),
                  pl.BlockSpec((None, hp, s, dqk), lambda bi, hi, qi: (bi, hi, 0, 0)),
                  pl.BlockSpec((None, hp, s, dv), lambda bi, hi, qi: (bi, hi, 0, 0))],
        out_specs=pl.BlockSpec((None, tq, hp * dv), lambda bi, hi, qi: (bi, qi, hi)),
        compiler_params=_params(("parallel", "parallel", "arbitrary")),
        name='mla_attention',
    )(q, k, v)


SUBLANES = 8


def _compress_body(x_ref, pos_ref, w1_ref, w2_ref, o_ref, sh_ref):
    n, half = x_ref.shape
    x = x_ref[...].astype(F32)
    first = jnp.dot((x + pos_ref[0:1, :]).astype(BF16), w1_ref[0:half, :],
                    preferred_element_type=F32)
    second = jnp.dot((x + pos_ref[1:2, :]).astype(BF16), w1_ref[half:2 * half, :],
                     preferred_element_type=F32)
    sh_ref[0:n, :] = second
    sh_ref[n:n + SUBLANES, :] = jnp.zeros((SUBLANES, sh_ref.shape[-1]), F32)
    hid = _silu(first + sh_ref[1:n + 1, :])
    o_ref[...] = jnp.dot(hid.astype(BF16), w2_ref[...], preferred_element_type=F32)


def _compress(t, pos_emb, w1, w2):
    assert CMP_LEN == 2 * CMP_STRIDE
    b, s, g, d = t.shape
    n_chunk = s // CMP_STRIDE
    half = CMP_STRIDE * d
    ch = t.reshape(b, n_chunk, CMP_STRIDE, g, d).transpose(0, 3, 1, 2, 4).reshape(b, g, n_chunk, half)
    w2p = jnp.zeros((CMP_HIDDEN, LANES), BF16).at[:, :d].set(w2.astype(BF16))
    out = pl.pallas_call(
        _compress_body,
        out_shape=jax.ShapeDtypeStruct((b, g, n_chunk, LANES), F32),
        grid=(b, g),
        in_specs=[pl.BlockSpec((None, None, n_chunk, half), lambda bi, gi: (bi, gi, 0, 0)),
                  pl.BlockSpec((2, half), lambda bi, gi: (0, 0)),
                  pl.BlockSpec((2 * half, CMP_HIDDEN), lambda bi, gi: (0, 0)),
                  pl.BlockSpec((CMP_HIDDEN, LANES), lambda bi, gi: (0, 0))],
        out_specs=pl.BlockSpec((None, None, n_chunk, LANES), lambda bi, gi: (bi, gi, 0, 0)),
        scratch_shapes=[pltpu.VMEM((n_chunk + SUBLANES, CMP_HIDDEN), F32)],
        compiler_params=_params(("parallel", "parallel")),
        name='nsa_compress',
    )(ch, pos_emb.reshape(2, half), w1.astype(BF16), w2p)
    return out[..., :d]


MASK_BIG = 1e30
NSA_GROUPS_PER_STEP = 1


def _nsa_body(q_ref, gl_ref, kct_ref, vc_ref, kse_ref, vs_ref, kwt_ref, vw_ref, o_ref, *,
              tq, tk, n_blocks, n_top, span):
    hg = NSA_HPG
    d = NSA_HEAD_DIM
    rows = hg * tq
    qi = pl.program_id(2)
    q0 = qi * tq
    gp = kct_ref.shape[0]
    qw = hg * d
    t_q = q0 + lax.broadcasted_iota(jnp.int32, (tq, 1), 0)

    def add_bias(s, bias):
        n = s.shape[-1]
        return (s.reshape(hg, tq, n) + bias[None]).reshape(rows, n)

    n_cmp = kct_ref.shape[-1]
    cmp_end = lax.broadcasted_iota(jnp.int32, (1, n_cmp), 1) * CMP_STRIDE + (CMP_LEN - 1)
    bias_c = jnp.where(cmp_end <= t_q, 0.0, NEG_INF)
    has_c = jnp.concatenate([t_q >= CMP_LEN - 1] * hg, axis=0)
    blk =lax.broadcasted_iota(jnp.int32, (LANES, LANES), 0)
    blk_f = blk.astype(F32)

    def before_loop(g):
        qf = q_ref[:, g * qw:(g + 1) * qw].astype(F32) * (d ** -0.5 * LOG2E)
        q = jnp.concatenate([qf[:, h * d:(h + 1) * d] for h in range(hg)], axis=0).astype(BF16)
        s_c = add_bias(jnp.dot(q, kct_ref[g], preferred_element_type=F32), bias_c)
        e_c = jnp.exp2(s_c - jnp.max(s_c, axis=-1, keepdims=True))
        acc_c = jnp.dot(e_c.astype(BF16), vc_ref[g], preferred_element_type=F32)
        inv_c = jnp.where(has_c, 1.0 / jnp.maximum(acc_c[:, d:d + 1], TINY), 0.0)
        imp = jnp.sum((acc_c[:, LANES:] * inv_c).reshape(hg, tq, LANES), axis=0)
        sel_parts = []
        for r in range(tq // LANES):
            imp_t = imp[r * LANES:(r + 1) * LANES].T
            cur = (q0 + r * LANES + lax.broadcasted_iota(jnp.int32, (1, LANES), 1)) // SLC_LEN
            forced = (blk == 0) | (blk == cur) | (blk == cur - 1)
            causal = blk <= cur
            imp_t = jnp.where(causal & jnp.logical_not(forced), imp_t, -2.0)
            sel_t = jnp.zeros((LANES, LANES), F32)
            for _ in range(n_top - N_FORCED):
                top = jnp.max(imp_t, axis=0, keepdims=True)
                first = jnp.min(jnp.where(imp_t == top, blk_f, float(LANES)), axis=0, keepdims=True)
                hit = blk_f == first
                sel_t = jnp.where(hit, 1.0, sel_t)
                imp_t = jnp.where(hit, -2.0, imp_t)
            sel_t = jnp.where(cur >= n_top, jnp.where(forced, 1.0, sel_t),
                              jnp.where(causal, 1.0, 0.0))
            sel_parts.append(sel_t.T)
        sel = jnp.concatenate(sel_parts, axis=0)
        unsel = jnp.concatenate([sel - 1.0] * hg, axis=0)
        q_aug = jnp.concatenate([unsel.astype(BF16), q], axis=1)
        return q, q_aug, acc_c, inv_c

    pre = [before_loop(g) for g in range(gp)]
    t_rel = t_q - lax.broadcasted_iota(jnp.int32, (1, tk), 1)

    def sel_tile(kt, carry, causal):
        ks = pl.multiple_of(kt * tk, tk)
        bias = jnp.where(t_rel >= ks, 0.0, NEG_INF) if causal else None
        out = []
        for g in range(gp):
            m, acc = carry[g]
            s = jnp.dot(pre[g][1], kse_ref[g, :, pl.ds(ks, tk)], preferred_element_type=F32)
            if causal:
                s = add_bias(s, bias)
            m_new = jnp.maximum(m, jnp.max(s, axis=-1, keepdims=True))
            p = jnp.exp2(s - m_new)
            acc = jnp.exp2(m - m_new) * acc + jnp.dot(p.astype(BF16), vs_ref[g, pl.ds(ks, tk), :],
                                                      preferred_element_type=F32)
            out.append((m_new, acc))
        return tuple(out)

    init = tuple((jnp.full((rows, 1), NEG_INF, F32), jnp.zeros((rows, vs_ref.shape[-1]), F32))
                 for _ in range(gp))
    n_kt = (q0 + tq + tk - 1) // tk
    carry = lax.fori_loop(0, n_kt - 1, lambda kt, c: sel_tile(kt, c, False), init)
    carry = sel_tile(n_kt - 1, carry, True)

    ws = pl.multiple_of(jnp.maximum(q0 + tq - span, 0), LANES)
    dist = (t_q - ws) - lax.broadcasted_iota(jnp.int32, (1, span), 1)
    bias_w = jnp.where((dist >= 0) & (dist < WIN), 0.0, NEG_INF)

    def after_loop(g):
        q, _, acc_c, inv_c = pre[g]
        acc_s = carry[g][1]
        inv_s = 1.0 / jnp.maximum(acc_s[:, d:d + 1], TINY)
        s_w = add_bias(jnp.dot(q, kwt_ref[g, :, pl.ds(ws, span)], preferred_element_type=F32),
                       bias_w)
        e_w = jnp.exp2(s_w - jnp.max(s_w, axis=-1, keepdims=True))
        acc_w = jnp.dot(e_w.astype(BF16), vw_ref[g, pl.ds(ws, span), :],
                        preferred_element_type=F32)
        inv_w = 1.0 / acc_w[:, d:d + 1]
        gates = _sigmoid(gl_ref[g])
        g_t = []
        for r in range(tq // LANES):
            blk_g = jnp.concatenate([gates[:, r * LANES:(r + 1) * LANES],
                                     jnp.zeros((LANES - gates.shape[0], LANES), F32)], axis=0)
            g_t.append(blk_g.T)
        g_t = jnp.concatenate(g_t, axis=0)

        def gate_col(branch):
            return jnp.concatenate([g_t[:, branch * hg + h:branch * hg + h + 1] for h in range(hg)],
                                   axis=0)

        o = ((gate_col(0) * inv_c) * acc_c[:, :d] + (gate_col(1) * inv_s) * acc_s[:, :d]
             + (gate_col(2) * inv_w) * acc_w[:, :d])
        o_ref[:, g * qw:(g + 1) * qw] = jnp.concatenate(
            [o[h * tq:(h + 1) * tq] for h in range(hg)], axis=1).astype(o_ref.dtype)

    for g in range(gp):
        after_loop(g)


def _nsa_attention(proj, gl_t, kct, vc, kst, vs, kwt, vw, *, tq=512, tk=512):
    b, s, _ = proj.shape
    g = NSA_GROUPS
    d = NSA_HEAD_DIM
    n_cmp = kct.shape[-1]
    tq = min(tq, s)
    tk = min(tk, s)
    span = min(WIN + tq, s)
    n_blocks = s // SLC_LEN
    gp = NSA_GROUPS_PER_STEP
    assert s % tq == 0 and s % tk == 0 and tq % LANES == 0 and n_blocks <= LANES and g % gp == 0
    kv_t = lambda n, r=d: pl.BlockSpec((None, gp, r, n), lambda bi, gi, qi: (bi, gi, 0, 0))
    kv_r = lambda n, w=LANES: pl.BlockSpec((None, gp, n, w), lambda bi, gi, qi: (bi, gi, 0, 0))

    def with_ones(v):
        tail = jnp.zeros(v.shape[:-1] + (LANES - d,), BF16).at[..., 0].set(1.0)
        return jnp.concatenate([v, tail], axis=-1)

    blk_of_key = jnp.arange(s, dtype=jnp.int32) // SLC_LEN
    mask_rows = jnp.where(jnp.arange(LANES, dtype=jnp.int32)[:, None] == blk_of_key[None, :],
                          MASK_BIG, 0.0).astype(BF16)
    kse = jnp.concatenate([jnp.broadcast_to(mask_rows, (b, g, LANES, s)), kst], axis=2)
    per = SLC_LEN // CMP_STRIDE
    jj = jnp.arange(n_cmp, dtype=jnp.int32)[:, None]
    nn = jnp.arange(LANES, dtype=jnp.int32)[None, :]
    pool = (0.5 * ((jj // per == nn).astype(F32) + ((jj + 1) // per == nn).astype(F32))).astype(BF16)
    vc = jnp.concatenate([with_ones(vc), jnp.broadcast_to(pool, (b, g, n_cmp, LANES))], axis=-1)
    vs, vw = with_ones(vs), with_ones(vw)
    kst = kse
    return pl.pallas_call(
        functools.partial(_nsa_body, tq=tq, tk=tk, n_blocks=n_blocks,
                          n_top=min(SLC_TOPN, n_blocks), span=span),
        out_shape=jax.ShapeDtypeStruct((b, s, g * NSA_HPG * d), BF16),
        grid=(b, g // gp, s // tq),
        in_specs=[pl.BlockSpec((None, tq, gp * NSA_HPG * d), lambda bi, gi, qi: (bi, qi, gi)),
                  pl.BlockSpec((None, gp, 16, tq), lambda bi, gi, qi: (bi, gi, 0, qi)),
                  kv_t(n_cmp), kv_r(n_cmp, 2 * LANES), kv_t(s, LANES + d), kv_r(s), kv_t(s),
                  kv_r(s)],
        out_specs=pl.BlockSpec((None, tq, gp * NSA_HPG * d), lambda bi, gi, qi: (bi, qi, gi)),
        compiler_params=_params(("parallel", "parallel", "arbitrary")),
        name='nsa_attention',
    )(proj, gl_t, kct, vc, kst, vs, kwt, vw)


def _pad_cols(w, n):
    return jnp.pad(w, ((0, 0), (0, n - w.shape[1])))


def _even_token_mixer(x, mod, norm_g, cos, sin, w_in, conv_w, a_log, dt_bias, dn_norm, q_norm,
                      kv_norm, w_uq, w_ukv, w_out, b, s):
    shift, scale, gate = mod
    n = b * s
    hd = DN_HEADS
    o_b = 4 * DN_W
    o_a = o_b + hd
    o_cq = o_a + hd
    o_ckv = o_cq + MLA_Q_RANK
    o_kr = o_ckv + MLA_KV_RANK
    w_cols = [w_in[:, :o_b], w_in[:, o_cq:o_ckv], w_in[:, o_ckv:o_kr],
              _pad_cols(w_in[:, o_kr:o_kr + MLA_ROPE], LANES),
              jnp.repeat(w_in[:, o_b:o_a], DN_HEAD_DIM, axis=1),
              jnp.repeat(w_in[:, o_a:o_cq], DN_HEAD_DIM, axis=1)]
    w_all = jnp.concatenate(w_cols, axis=1)
    proj = _mm(x, w_all, pro='adaln', pro_args=(norm_g, scale, shift), rows_per_batch=s,
               tn_cap=4096, name='even_in_proj')
    width = proj.shape[1]
    c0 = 4 * DN_W
    pb_blk = (c0 + MLA_Q_RANK + MLA_KV_RANK + LANES) // LANES
    proj3 = proj.reshape(b, s, width)

    o_dn = _deltanet(proj3, conv_w.astype(F32), a_log, dt_bias, dn_norm, z_blk=3 * hd,
                     pb_blk=pb_blk, pa_blk=pb_blk + hd)

    reps = LANES // (MLA_ROPE // 2)
    cos_t = jnp.tile(cos.reshape(n, -1), (1, reps))
    sin_t = jnp.tile(sin.reshape(n, -1), (1, reps))
    q_h = _mla_q_up(proj, q_norm, w_uq, cos_t, sin_t, b, s, cq_blk=c0 // MLA_Q_RANK)
    k_h, v_h = _mla_kv_up(proj, kv_norm, w_ukv, cos_t, sin_t, b, s,
                          ckv_blk=(c0 + MLA_Q_RANK) // LANES,
                          kr_blk=(c0 + MLA_Q_RANK + MLA_KV_RANK) // LANES)
    o_mla = _mla_attention(q_h, k_h, v_h)
    mix = jnp.concatenate([o_dn, o_mla], axis=-1).reshape(n, -1)
    return _mm(mix, w_out, epi='residual', epi_args=(x, gate), rows_per_batch=s, name='even_out_proj')


def _odd_token_mixer(x, mod, norm_g, w_in, pos_k, pos_v, ck1, ck2, cv1, cv2, w_out, b, s):
    shift, scale, gate = mod
    n = b * s
    g = NSA_GROUPS
    d = NSA_HEAD_DIM
    n_in = NSA_Q_W + 6 * NSA_KV_W + 3 * NSA_HEADS
    w_all = _pad_cols(w_in, -(-n_in // LANES) * LANES)
    proj = _mm(x, w_all, pro='adaln', pro_args=(norm_g, scale, shift), rows_per_batch=s,
               tn_cap=4096, out_dtype=BF16, name='odd_in_proj')
    proj3 = proj.reshape(b, s, -1)

    def kv(i):
        lo = NSA_Q_W + i * NSA_KV_W
        return proj3[:, :, lo:lo + NSA_KV_W].reshape(b, s, g, d)

    k_cmp = _compress(kv(0), pos_k, ck1, ck2)
    v_cmp = _compress(kv(1), pos_v, cv1, cv2)
    to_t = lambda t: t.transpose(0, 2, 3, 1).astype(BF16)
    to_r = lambda t: t.transpose(0, 2, 1, 3).astype(BF16)
    gl = proj3[:, :, NSA_Q_W + 6 * NSA_KV_W:n_in].reshape(b, s, g, NSA_HPG, 3)
    gl_t = gl.transpose(0, 2, 4, 3, 1).reshape(b, g, 3 * NSA_HPG, s)
    gl_t = jnp.pad(gl_t.astype(F32), ((0, 0), (0, 0), (0, 16 - 3 * NSA_HPG), (0, 0)))
    o = _nsa_attention(proj3, gl_t, k_cmp.transpose(0, 1, 3, 2).astype(BF16), v_cmp.astype(BF16),
                       to_t(kv(2)), to_r(kv(3)), to_t(kv(4)), to_r(kv(5)))
    return _mm(o.reshape(n, -1), w_out, epi='residual', epi_args=(x, gate), rows_per_batch=s,
               name='odd_out_proj')


def _moe(x, mod, norm_g, w_router, b_router, w1, w3, w2, layer, b, s):
    shift, scale, gate = mod
    n = b * s
    h, logits = _norm_router(x, norm_g, scale, shift, w_router, s)
    logits = logits[:, :N_EXPERTS] + b_router.astype(F32)
    top_val, top_idx = lax.top_k(logits, TOP_K)
    gate_w = jax.nn.softmax(top_val, axis=-1)
    flat_e = top_idx.reshape(-1)
    onehot = (flat_e[:, None] == jnp.arange(N_EXPERTS, dtype=jnp.int32)[None, :]).astype(jnp.int32)
    rank = jnp.take_along_axis(jnp.cumsum(onehot, axis=0), flat_e[:, None], axis=1)[:, 0] - 1
    counts = jnp.sum(onehot, axis=0)
    padded = ((counts + MOE_BLOCK - 1) // MOE_BLOCK) * MOE_BLOCK
    pad_end = jnp.cumsum(padded)
    pad_start = pad_end - padded
    dest = pad_start[flat_e] + rank
    n_assign = n * TOP_K
    n_blk = -(-n_assign // MOE_BLOCK) + N_EXPERTS
    n_rows = n_blk * MOE_BLOCK
    row_assign = jnp.zeros((n_rows,), jnp.int32).at[dest].set(
        jnp.arange(n_assign, dtype=jnp.int32), unique_indices=True)
    row_tok = row_assign // TOP_K
    blk_exp = jnp.minimum(jnp.searchsorted(pad_end, jnp.arange(n_blk, dtype=jnp.int32) * MOE_BLOCK,
                                           side='right'), N_EXPERTS - 1).astype(jnp.int32)
    n_used = (pad_end[-1:] // MOE_BLOCK).astype(jnp.int32)
    row_w = gate_w.reshape(-1)[row_assign]
    y_rows = _swiglu_grouped(h[row_tok], row_w[:, None], blk_exp, n_used, w1, w3, w2, layer)
    dest_tok = dest.reshape(n, TOP_K)
    return _moe_combine(x, [y_rows[dest_tok[:, j]] for j in range(TOP_K)], gate, s)


def kernel(x, c, positions, ada_w, ada_b, norm_g, final_g, ev_w_in, ev_conv_w, ev_a_log, ev_dt_bias, ev_dn_norm, ev_q_norm, ev_kv_norm, ev_w_uq, ev_w_ukv, ev_w_out, ev_ff_gate, ev_ff_up, ev_ff_down, od_w_in, od_cmp_pos_k, od_cmp_pos_v, od_cmp_k1, od_cmp_k2, od_cmp_v1, od_cmp_v2, od_w_out, od_router, od_router_b, od_moe_w1, od_moe_w3, od_moe_w2):
    b, s, dm = x.shape
    depth = ada_w.shape[0]
    inv = 1.0 / (ROPE_THETA ** (jnp.arange(0, MLA_ROPE, 2, dtype=F32) / MLA_ROPE))
    ang = positions.astype(F32)[..., None] * inv
    cos, sin = jnp.cos(ang), jnp.sin(ang)
    mods = jnp.einsum('bd,lkde->lkbe', jax.nn.silu(c), ada_w,
                      precision=lax.Precision.HIGHEST) + ada_b[:, :, None, :]

    def mod(layer, k):
        m = mods[layer, k]
        return tuple(m[:, None, i * dm:(i + 1) * dm] for i in range(3))

    moe_w1, moe_w3, moe_w2 = (_cast_bf16(w) for w in (od_moe_w1, od_moe_w3, od_moe_w2))
    xf = x.reshape(b * s, dm)
    for layer in range(depth):
        j = layer // 2
        if layer % 2 == 0:
            xf = _even_token_mixer(xf, mod(layer, 0), norm_g[layer, 0], cos, sin, ev_w_in[j],
                                   ev_conv_w[j], ev_a_log[j], ev_dt_bias[j], ev_dn_norm[j],
                                   ev_q_norm[j], ev_kv_norm[j], ev_w_uq[j], ev_w_ukv[j],
                                   ev_w_out[j], b, s)
            shift, scale, gate = mod(layer, 1)
            xf = _swiglu_dense(xf, norm_g[layer, 1], scale, shift, ev_ff_gate[j], ev_ff_up[j],
                               ev_ff_down[j], gate, s)
        else:
            xf = _odd_token_mixer(xf, mod(layer, 0), norm_g[layer, 0], od_w_in[j],
                                  od_cmp_pos_k[j], od_cmp_pos_v[j], od_cmp_k1[j], od_cmp_k2[j],
                                  od_cmp_v1[j], od_cmp_v2[j], od_w_out[j], b, s)
            xf = _moe(xf, mod(layer, 1), norm_g[layer, 1], od_router[j], od_router_b[j],
                      moe_w1, moe_w3, moe_w2, j, b, s)
    return _final_norm(xf, final_g).reshape(b, s, dm)
```
